```python
import jax
import jax.numpy as jnp
from jax import lax
import numpy as np

D_MODEL = 2048
BATCH = 8
SEQ = 2048
DEPTH = 2
DEC_BATCH = 128
DEC_SEQ = 4
PAST_LEN = 2048
PAGE_SIZE = 128

F32 = jnp.float32
GROUP_W = D_MODEL // 4
RET_HEADS = 4
RET_DV = GROUP_W // RET_HEADS
RET_DK = RET_DV // 2
RET_CHUNK = 128
SB_HEADS = 4
SB_HD = GROUP_W // SB_HEADS
SB_BLOCK = 128
DSA_HEADS = 4
DSA_HD = GROUP_W // DSA_HEADS
IDX_HEADS = 8
IDX_HD = 64
DSA_TOPK_MAX = 256
DSA_BLOCK = 128
DSA_CACHE_W = 2 * DSA_HD + IDX_HD
RW_HD = 64
RW_HEADS = GROUP_W // RW_HD
RW_DECAY_LORA = 96
RW_A_LORA = 96
SHIFT_W = 3 * GROUP_W + RW_DECAY_LORA + RW_A_LORA
RW_LN_EPS = 64e-5
NORM_EPS = 1e-6
ROPE_BASE = 10000.0
COL_SIZES = (
    ('ret_q', RET_HEADS * RET_DK), ('ret_k', RET_HEADS * RET_DK), ('ret_v', GROUP_W), ('ret_g', GROUP_W),
    ('sb_q', GROUP_W), ('sb_k', GROUP_W), ('sb_v', GROUP_W), ('sb_g', GROUP_W),
    ('dsa_q', GROUP_W), ('dsa_k', DSA_HD), ('dsa_v', DSA_HD), ('idx_q', IDX_HEADS * IDX_HD),
    ('idx_k', IDX_HD), ('idx_w', IDX_HEADS), ('dsa_g', GROUP_W),
    ('rw_shift', SHIFT_W), ('rw_g', GROUP_W),
)
IN_COLS = sum(n for _, n in COL_SIZES)

kernel_name = 'hybrid_ret_sb_dsa_rwkv7_step'


def _rmsnorm(x, g):
    xf = x.astype(F32)
    y = xf * lax.rsqrt(jnp.mean(xf * xf, axis=-1, keepdims=True) + NORM_EPS)
    return (y * g.astype(F32)).astype(x.dtype)


def _split_cols(p):
    out, off = {}, 0
    for name, n in COL_SIZES:
        out[name] = p[..., off:off + n]
        off += n
    return out


def _blocks(t, blk):
    n, L = t.shape[:2]
    return jnp.moveaxis(t.reshape(n, L // blk, blk, *t.shape[2:]), 1, 0)


def _unblocks(t):
    t = jnp.moveaxis(t, 0, 1)
    return t.reshape(t.shape[0], -1, *t.shape[3:])


def _rotary(x, pos):
    half = x.shape[-1] // 2
    inv = 1.0 / (ROPE_BASE ** (jnp.arange(half, dtype=F32) / half))
    ang = pos.astype(F32)[:, None] * inv[None, :]
    cos = jnp.cos(ang)[None, :, None, :]
    sin = jnp.sin(ang)[None, :, None, :]
    x1, x2 = x[..., :half], x[..., half:]
    return jnp.concatenate([x1 * cos - x2 * sin, x1 * sin + x2 * cos], axis=-1)


def _retention(q, k, v, pos, s0):
    n, L = q.shape[:2]
    chunk = RET_CHUNK if L % RET_CHUNK == 0 else L
    q = _rotary(q.astype(F32), pos)
    k = _rotary(k.astype(F32), pos) * (RET_DK ** -0.5)
    v = v.astype(F32)
    log_g = jnp.log1p(-jnp.exp2(-5.0 - jnp.arange(RET_HEADS, dtype=F32)))
    i = jnp.arange(chunk, dtype=F32)
    diff = i[:, None] - i[None, :]
    dmask = jnp.where(diff[None] >= 0, jnp.exp(jnp.maximum(diff, 0.0)[None] * log_g[:, None, None]), 0.0)
    q_dec = jnp.exp((i + 1.0)[:, None] * log_g[None, :])
    k_dec = jnp.exp((chunk - 1.0 - i)[:, None] * log_g[None, :])
    s_dec = jnp.exp(chunk * log_g)

    def step(S, inp):
        qc, kc, vc = inp
        att = jnp.einsum('nqhd,nkhd->nhqk', qc, kc) * dmask[None]
        o = (jnp.einsum('nhqk,nkhe->nqhe', att, vc)
             + jnp.einsum('nqhd,nhde->nqhe', qc, S) * q_dec[None, :, :, None])
        S = S * s_dec[None, :, None, None] + jnp.einsum('nkhd,nkhe->nhde', kc * k_dec[None, :, :, None], vc)
        return S, o

    S, o = lax.scan(step, s0.astype(F32), (_blocks(q, chunk), _blocks(k, chunk), _blocks(v, chunk)))
    o = _unblocks(o)
    o = o * lax.rsqrt(jnp.mean(o * o, axis=-1, keepdims=True) + NORM_EPS)
    return o.reshape(n, L, RET_HEADS * RET_DV), S


def _stick_breaking(q, k, v, q_pos, k_pos):
    n, Lq, H, d = q.shape
    blk = SB_BLOCK if Lq % SB_BLOCK == 0 else Lq
    k = k.astype(F32)
    v = v.astype(F32)
    scale = d ** -0.5

    def one(args):
        qi, pi = args
        z = jnp.einsum('nqhd,nkhd->nhqk', qi, k) * scale
        vis = (k_pos[None, :] < pi[:, None])[None, None]
        log_1mb = jnp.where(vis, jax.nn.log_sigmoid(-z), 0.0)
        rev = lax.cumsum(log_1mb, axis=3, reverse=True)
        after = jnp.concatenate([rev[..., 1:], jnp.zeros_like(rev[..., :1])], axis=-1)
        w = jnp.where(vis, jnp.exp(jax.nn.log_sigmoid(z) + after), 0.0)
        return jnp.einsum('nhqk,nkhd->nqhd', w, v)

    o = lax.map(one, (_blocks(q.astype(F32), blk), q_pos.reshape(Lq // blk, blk)))
    return _unblocks(o)


def _dsa(q, k, v, iq, ik, iw, q_pos, k_pos):
    n, Lq, H, d = q.shape
    Lk = k.shape[1]
    topk = min(DSA_TOPK_MAX, Lk // 4)
    blk = DSA_BLOCK if Lq % DSA_BLOCK == 0 else Lq
    k = k.astype(F32)
    v = v.astype(F32)
    ik = ik.astype(F32)
    gather = jax.vmap(lambda rows, idx: rows[idx])

    def one(args):
        qi, iqi, iwi, pi = args
        rel = jax.nn.relu(jnp.einsum('nqhe,nke->nqhk', iqi, ik) * (IDX_HD ** -0.5))
        score = jnp.einsum('nqhk,nqh->nqk', rel, iwi * (IDX_HEADS ** -0.5))
        vis = (k_pos[None, :] <= pi[:, None])[None]
        score = jnp.where(vis, score, -jnp.inf)
        _, idx = lax.top_k(score, topk)
        kg = gather(k, idx)
        vg = gather(v, idx)
        ok = (k_pos[idx] <= pi[None, :, None])[:, None]
        logits = jnp.einsum('nqhd,nqkd->nhqk', qi, kg) * (d ** -0.5)
        p = jax.nn.softmax(jnp.where(ok, logits, -jnp.inf), axis=-1)
        return jnp.einsum('nhqk,nqkd->nqhd', p, vg)

    o = lax.map(one, (_blocks(q.astype(F32), blk), _blocks(iq.astype(F32), blk),
                      _blocks(iw.astype(F32), blk), q_pos.reshape(Lq // blk, blk)))
    return _unblocks(o)


def _rwkv7(xs, s0, w0, w2, a0, a2, k_k, k_a, r_k, lnx_w, lnx_b):
    n, L, _ = xs.shape
    xs = xs.astype(F32)
    o1 = GROUP_W
    o2 = o1 + RW_DECAY_LORA
    o3 = o2 + GROUP_W
    o4 = o3 + GROUP_W
    r, wl, k, v, al = xs[..., :o1], xs[..., o1:o2], xs[..., o2:o3], xs[..., o3:o4], xs[..., o4:]
    w_raw = -jax.nn.softplus(-(w0 + jnp.tanh(wl) @ w2)) - 0.5
    decay = jnp.exp(-jnp.exp(w_raw))
    a = jax.nn.sigmoid(a0 + al @ a2)
    kk = k * k_k
    k = k * (1.0 + (a - 1.0) * k_a)
    hs = lambda t: t.reshape(n, L, RW_HEADS, RW_HD)
    r, decay, k, v, kk, a = hs(r), hs(decay), hs(k), hs(v), hs(kk), hs(a)
    kk = kk / jnp.maximum(jnp.sqrt(jnp.sum(kk * kk, axis=-1, keepdims=True)), 1e-12)

    def step(S, inp):
        rt, wt, kt, vt, kkt, at = inp
        sa = jnp.einsum('nhij,nhj->nhi', S, -kkt)
        S = (S * wt[:, :, None, :] + sa[..., None] * (kkt * at)[:, :, None, :]
             + vt[..., None] * kt[:, :, None, :])
        return S, jnp.einsum('nhij,nhj->nhi', S, rt)

    tm = lambda t: jnp.moveaxis(t, 1, 0)
    S, y = lax.scan(step, s0.astype(F32), (tm(r), tm(decay), tm(k), tm(v), tm(kk), tm(a)))
    y = jnp.moveaxis(y, 0, 1)
    mu = jnp.mean(y, axis=-1, keepdims=True)
    var = jnp.mean((y - mu) ** 2, axis=-1, keepdims=True)
    y = (y - mu) * lax.rsqrt(var + RW_LN_EPS)
    y = y * lnx_w.reshape(RW_HEADS, RW_HD) + lnx_b.reshape(RW_HEADS, RW_HD)
    y = y + jnp.sum(r * k * r_k, axis=-1, keepdims=True) * v
    return y.reshape(n, L, GROUP_W), S


def _layer(x, past, prm):
    shift_prev, ret_s0, wkv_s0, sb_past, dsa_past = past
    g, w_in, mu, w0, w2, a0, a2, k_k, k_a, r_k, lnx_w, lnx_b, w_out = prm
    n, L, _ = x.shape
    past_len = sb_past.shape[1]
    q_pos = past_len + jnp.arange(L, dtype=jnp.int32)
    k_pos = jnp.arange(past_len + L, dtype=jnp.int32)
    h = _rmsnorm(x, g)
    p = _split_cols(jnp.einsum('nld,dc->nlc', h, w_in))
    ret_o, ret_s = _retention(p['ret_q'].reshape(n, L, RET_HEADS, RET_DK),
                              p['ret_k'].reshape(n, L, RET_HEADS, RET_DK),
                              p['ret_v'].reshape(n, L, RET_HEADS, RET_DV), q_pos, ret_s0)
    sb_new = jnp.stack([p['sb_k'], p['sb_v']], axis=2).reshape(n, L, 2, SB_HEADS, SB_HD)
    sb_all = jnp.concatenate([sb_past.astype(sb_new.dtype), sb_new], axis=1)
    sb_o = _stick_breaking(p['sb_q'].reshape(n, L, SB_HEADS, SB_HD), sb_all[:, :, 0], sb_all[:, :, 1],
                           q_pos, k_pos).reshape(n, L, GROUP_W)
    dsa_new = jnp.concatenate([p['dsa_k'], p['dsa_v'], p['idx_k']], axis=-1)
    dsa_all = jnp.concatenate([dsa_past.astype(dsa_new.dtype), dsa_new], axis=1)
    dsa_o = _dsa(p['dsa_q'].reshape(n, L, DSA_HEADS, DSA_HD), dsa_all[..., :DSA_HD],
                 dsa_all[..., DSA_HD:2 * DSA_HD], p['idx_q'].reshape(n, L, IDX_HEADS, IDX_HD),
                 dsa_all[..., 2 * DSA_HD:], p['idx_w'], q_pos, k_pos).reshape(n, L, GROUP_W)
    ps = p['rw_shift']
    prev = jnp.concatenate([shift_prev[:, None].astype(ps.dtype), ps[:, :-1]], axis=1)
    rw_o, wkv_s = _rwkv7(ps + (prev - ps) * mu, wkv_s0, w0, w2, a0, a2, k_k, k_a, r_k, lnx_w, lnx_b)
    silu = lambda t: jax.nn.silu(t.astype(F32))
    mix = jnp.concatenate([ret_o * silu(p['ret_g']), sb_o * silu(p['sb_g']),
                           dsa_o * silu(p['dsa_g']), rw_o * silu(p['rw_g'])], axis=-1)
    x = x + jnp.einsum('nlc,cd->nld', mix, w_out)
    return x, (sb_new, dsa_new, ret_s, wkv_s, ps[:, -1])


def _gather_pages(cache_l, page_table):
    g = cache_l[page_table]
    return g.reshape(g.shape[0], g.shape[1] * g.shape[2], *g.shape[3:])


def setup_inputs(seed: int = 0) -> dict:
    key = jax.random.key(seed)
    ks = jax.random.split(key, 24)
    n_pages = PAST_LEN // PAGE_SIZE
    n_used = DEC_BATCH * n_pages
    n_pool = n_used + max(1, n_used // 4)
    nrm = lambda k, s, sc: sc * jax.random.normal(k, s, F32)
    x_prompt = nrm(ks[0], (BATCH, SEQ, D_MODEL), 1.0)
    x_sample = nrm(ks[1], (DEC_BATCH, DEC_SEQ, D_MODEL), 1.0)
    cache_sb_kv = nrm(ks[2], (DEPTH, n_pool, PAGE_SIZE, 2, SB_HEADS, SB_HD), 1.0)
    cache_dsa = nrm(ks[3], (DEPTH, n_pool, PAGE_SIZE, DSA_CACHE_W), 1.0)
    state_ret = nrm(ks[4], (DEPTH, DEC_BATCH, RET_HEADS, RET_DK, RET_DV), 0.5)
    state_wkv = nrm(ks[5], (DEPTH, DEC_BATCH, RW_HEADS, RW_HD, RW_HD), 0.3)
    state_shift = nrm(ks[6], (DEPTH, DEC_BATCH, SHIFT_W), 1.0)
    page_table = jax.random.permutation(ks[7], n_pool)[:n_used].reshape(DEC_BATCH, n_pages).astype(jnp.int32)
    norm_g = 1.0 + nrm(ks[8], (DEPTH, D_MODEL), 0.02)
    w_in = nrm(ks[9], (DEPTH, D_MODEL, IN_COLS), D_MODEL ** -0.5)
    shift_mu = jax.random.uniform(ks[10], (DEPTH, SHIFT_W), F32)
    rw_w0 = jax.random.uniform(ks[11], (DEPTH, GROUP_W), F32, -3.0, 0.0)
    rw_w2 = nrm(ks[12], (DEPTH, RW_DECAY_LORA, GROUP_W), 0.5 * RW_DECAY_LORA ** -0.5)
    rw_a0 = nrm(ks[13], (DEPTH, GROUP_W), 0.1)
    rw_a2 = nrm(ks[14], (DEPTH, RW_A_LORA, GROUP_W), 0.5 * RW_A_LORA ** -0.5)
    rw_kk = 0.85 + nrm(ks[15], (DEPTH, GROUP_W), 0.05)
    rw_ka = 1.0 + nrm(ks[16], (DEPTH, GROUP_W), 0.05)
    rw_rk = nrm(ks[17], (DEPTH, RW_HEADS, RW_HD), 0.1)
    rw_lnx_w = 1.0 + nrm(ks[18], (DEPTH, GROUP_W), 0.02)
    rw_lnx_b = nrm(ks[19], (DEPTH, GROUP_W), 0.01)
    w_out = nrm(ks[20], (DEPTH, D_MODEL, D_MODEL), D_MODEL ** -0.5)
    final_g = 1.0 + nrm(ks[21], (D_MODEL,), 0.02)
    return {'x_prompt': x_prompt, 'x_sample': x_sample, 'cache_sb_kv': cache_sb_kv, 'cache_dsa': cache_dsa,
            'state_ret': state_ret, 'state_wkv': state_wkv, 'state_shift': state_shift, 'page_table': page_table,
            'norm_g': norm_g, 'w_in': w_in, 'shift_mu': shift_mu, 'rw_w0': rw_w0, 'rw_w2': rw_w2,
            'rw_a0': rw_a0, 'rw_a2': rw_a2, 'rw_kk': rw_kk, 'rw_ka': rw_ka, 'rw_rk': rw_rk,
            'rw_lnx_w': rw_lnx_w, 'rw_lnx_b': rw_lnx_b, 'w_out': w_out, 'final_g': final_g}


def reference(x_prompt, x_sample, cache_sb_kv, cache_dsa, state_ret, state_wkv, state_shift, page_table,
              norm_g, w_in, shift_mu, rw_w0, rw_w2, rw_a0, rw_a2, rw_kk, rw_ka, rw_rk, rw_lnx_w, rw_lnx_b,
              w_out, final_g):
    n_p = x_prompt.shape[0]
    dt = x_prompt.dtype
    hp, hs = x_prompt, x_sample
    p_sb, p_dsa, p_ret, p_wkv, p_shift = [], [], [], [], []
    s_sb, s_dsa, s_ret, s_wkv, s_shift = [], [], [], [], []
    for l in range(DEPTH):
        prm = (norm_g[l], w_in[l], shift_mu[l], rw_w0[l], rw_w2[l], rw_a0[l], rw_a2[l], rw_kk[l], rw_ka[l],
               rw_rk[l], rw_lnx_w[l], rw_lnx_b[l], w_out[l])
        past_p = (jnp.zeros((n_p, SHIFT_W), dt), jnp.zeros((n_p, RET_HEADS, RET_DK, RET_DV), F32),
                  jnp.zeros((n_p, RW_HEADS, RW_HD, RW_HD), F32), jnp.zeros((n_p, 0, 2, SB_HEADS, SB_HD), dt),
                  jnp.zeros((n_p, 0, DSA_CACHE_W), dt))
        hp, st = _layer(hp, past_p, prm)
        p_sb.append(st[0]); p_dsa.append(st[1]); p_ret.append(st[2]); p_wkv.append(st[3]); p_shift.append(st[4])
        past_s = (state_shift[l], state_ret[l], state_wkv[l], _gather_pages(cache_sb_kv[l], page_table),
                  _gather_pages(cache_dsa[l], page_table))
        hs, st = _layer(hs, past_s, prm)
        s_sb.append(st[0]); s_dsa.append(st[1]); s_ret.append(st[2]); s_wkv.append(st[3]); s_shift.append(st[4])
    y_prompt = _rmsnorm(hp, final_g)
    y_sample = _rmsnorm(hs, final_g)
    return (y_prompt, y_sample,
            jnp.stack(p_sb), jnp.stack(p_dsa), jnp.stack(p_ret), jnp.stack(p_wkv), jnp.stack(p_shift),
            jnp.stack(s_sb), jnp.stack(s_dsa), jnp.stack(s_ret), jnp.stack(s_wkv), jnp.stack(s_shift))
```

```python
import functools
import math

import jax
import jax.numpy as jnp
import numpy as np
from jax import lax
from jax.experimental import pallas as pl
from jax.experimental.pallas import tpu as pltpu

F32 = jnp.float32
BF16 = jnp.bfloat16
LANE = 128
SUBLANE = 8

GROUP_W = 512
RET_HEADS, RET_DK, RET_DV = 4, 64, 128
SB_HEADS, SB_HD = 4, 128
DSA_HEADS, DSA_HD = 4, 128
IDX_HEADS, IDX_HD = 8, 64
DSA_TOPK_MAX = 256
DSA_CACHE_W = 2 * DSA_HD + IDX_HD
RW_HD, RW_HEADS = 64, 8
RW_LORA = 96
SHIFT_W = 3 * GROUP_W + 2 * RW_LORA
RW_LN_EPS = 64e-5
NORM_EPS = 1e-6
ROPE_BASE = 10000.0
BLK = 128

_COL = dict(ret_q=0, ret_k=2, ret_v=4, ret_g=8, sb_q=12, sb_k=16, sb_v=20, sb_g=24,
            dsa_q=28, idx_q=32, dsa_g=36, dsa_k=40, dsa_v=41, idx_k=42, idx_w=43,
            rw_g=44, rw_shift=48)
N_COLS = 64 * LANE
SHIFT_P = 16 * LANE
NT = (((1,), (1,)), ((), ()))
TN = (((0,), (0,)), ((), ()))
INT_MIN = -2 ** 31


def _silu(x):
    return x * jax.nn.sigmoid(x)


def _rmsnorm_body(x_ref, g_ref, o_ref):
    x = x_ref[...]
    y = x * lax.rsqrt(jnp.mean(x * x, axis=-1, keepdims=True) + NORM_EPS)
    o_ref[...] = (y * g_ref[...]).astype(o_ref.dtype)


def _rmsnorm(x, g, out_dtype, tm=256):
    t, d = x.shape
    tm = min(tm, t)
    return pl.pallas_call(
        _rmsnorm_body, grid=(t // tm,),
        in_specs=[pl.BlockSpec((tm, d), lambda i: (i, 0)), pl.BlockSpec((1, d), lambda i: (0, 0))],
        out_specs=pl.BlockSpec((tm, d), lambda i: (i, 0)),
        out_shape=jax.ShapeDtypeStruct((t, d), out_dtype), name="rmsnorm")(x, g.reshape(1, d))


def _matmul_body(a_ref, b_ref, o_ref):
    o_ref[...] = jnp.dot(a_ref[...], b_ref[...], preferred_element_type=F32)


def _in_proj(h, w, tm=512, tn=1024):
    t, d = h.shape
    nc = w.shape[1]
    tm = min(tm, t)
    return pl.pallas_call(
        _matmul_body, grid=(t // tm, nc // tn),
        in_specs=[pl.BlockSpec((tm, d), lambda i, j: (i, 0)), pl.BlockSpec((d, tn), lambda i, j: (0, j))],
        out_specs=pl.BlockSpec((tm, tn), lambda i, j: (i, j)),
        out_shape=jax.ShapeDtypeStruct((t, nc), F32), name="in_proj")(h, w)


def _out_proj_body(x_ref, m_ref, w_ref, g_ref, xo_ref, yo_ref):
    xn = x_ref[...] + jnp.dot(m_ref[...], w_ref[...], preferred_element_type=F32)
    xo_ref[...] = xn
    y = xn * lax.rsqrt(jnp.mean(xn * xn, axis=-1, keepdims=True) + NORM_EPS)
    yo_ref[...] = (y * g_ref[...]).astype(yo_ref.dtype)


def _out_proj(x, mix, w, g, y_dtype, tm=256):
    t, d = x.shape
    tm = min(tm, t)
    return pl.pallas_call(
        _out_proj_body, grid=(t // tm,),
        in_specs=[pl.BlockSpec((tm, d), lambda i: (i, 0)), pl.BlockSpec((tm, d), lambda i: (i, 0)),
                  pl.BlockSpec((d, d), lambda i: (0, 0)), pl.BlockSpec((1, d), lambda i: (0, 0))],
        out_specs=[pl.BlockSpec((tm, d), lambda i: (i, 0)), pl.BlockSpec((tm, d), lambda i: (i, 0))],
        out_shape=[jax.ShapeDtypeStruct((t, d), F32), jax.ShapeDtypeStruct((t, d), y_dtype)],
        name="out_proj")(x, mix, w, g.reshape(1, d))


def _ret_body(q_ref, k_ref, v_ref, g_ref, cos_ref, sin_ref, dmask_ref, qdec_ref, kdec_ref, sdec_ref,
              s0_ref, o_ref, sout_ref, ss_ref, *, n_seq, seq_len):
    rows = n_seq * seq_len

    @pl.when(pl.program_id(1) == 0)
    def _():
        ss_ref[...] = s0_ref[...]

    lane = lax.broadcasted_iota(jnp.int32, (rows, RET_HEADS * RET_DK), 1)
    first_half = (lane % RET_DK) < (RET_DK // 2)
    cosf, sinf = cos_ref[...], sin_ref[...]
    width = RET_HEADS * RET_DK

    def rot(x):
        swapped = jnp.where(first_half, pltpu.roll(x, width - RET_DK // 2, 1), pltpu.roll(x, RET_DK // 2, 1))
        return x * cosf + swapped * sinf

    qr = rot(q_ref[...])
    kr = rot(k_ref[...]) * (RET_DK ** -0.5)
    kr_b = kr.astype(BF16)
    kd = kr * kdec_ref[...]
    v = v_ref[...]
    g = g_ref[...]
    qdec = qdec_ref[...]
    upd = [jnp.zeros((width, RET_DV), F32) for _ in range(n_seq)]
    for h in range(RET_HEADS):
        mh = (lane // RET_DK) == h
        qh = jnp.where(mh, qr, 0.0).astype(BF16)
        att = lax.dot_general(qh, kr_b, NT, preferred_element_type=F32) * dmask_ref[h]
        vh = v[:, h * RET_DV:(h + 1) * RET_DV].astype(BF16)
        o = jnp.dot(att.astype(BF16), vh, preferred_element_type=F32)
        inter = [jnp.dot(qh[s * seq_len:(s + 1) * seq_len], ss_ref[s].astype(BF16), preferred_element_type=F32)
                 for s in range(n_seq)]
        inter = inter[0] if n_seq == 1 else jnp.concatenate(inter, axis=0)
        o = o + inter * qdec[:, h * RET_DV:(h + 1) * RET_DV]
        o = o * lax.rsqrt(jnp.mean(o * o, axis=-1, keepdims=True) + NORM_EPS)
        o_ref[:, h * RET_DV:(h + 1) * RET_DV] = (o * _silu(g[:, h * RET_DV:(h + 1) * RET_DV])).astype(o_ref.dtype)
        kdh = jnp.where(mh, kd, 0.0).astype(BF16)
        for s in range(n_seq):
            upd[s] = upd[s] + lax.dot_general(kdh[s * seq_len:(s + 1) * seq_len], vh[s * seq_len:(s + 1) * seq_len],
                                              TN, preferred_element_type=F32)
    sdec = sdec_ref[...]
    for s in range(n_seq):
        new = ss_ref[s] * sdec + upd[s]
        ss_ref[s] = new
        sout_ref[s] = new


def _rotary_tables(pos_abs):
    half = RET_DK // 2
    inv = 1.0 / (ROPE_BASE ** (jnp.arange(half, dtype=F32) / half))
    ang = jnp.asarray(pos_abs, F32)[:, None] * inv[None, :]
    cos, sin = jnp.cos(ang), jnp.sin(ang)
    cosf = jnp.tile(jnp.concatenate([cos, cos], axis=1), (1, RET_HEADS))
    sinf = jnp.tile(jnp.concatenate([-sin, sin], axis=1), (1, RET_HEADS))
    return cosf, sinf


def _decay_tables(pos_in_seq, seq_id, valid_len):
    log_g = jnp.log1p(-jnp.exp2(-5.0 - jnp.arange(RET_HEADS, dtype=F32)))
    i = jnp.asarray(pos_in_seq, F32)
    sid = jnp.asarray(seq_id)
    diff = i[:, None] - i[None, :]
    same = sid[:, None] == sid[None, :]
    dmask = jnp.where((diff >= 0)[None] & same[None],
                      jnp.exp(jnp.maximum(diff, 0.0)[None] * log_g[:, None, None]), 0.0)
    q_dec = jnp.exp((i + 1.0)[:, None] * log_g[None, :])
    k_dec = jnp.where((i < valid_len)[:, None], jnp.exp((valid_len - 1.0 - i)[:, None] * log_g[None, :]), 0.0)
    s_dec = jnp.exp(valid_len * log_g)
    q_dec = jnp.repeat(q_dec, RET_DV, axis=1)
    k_dec = jnp.repeat(k_dec, RET_DK, axis=1)
    s_dec = jnp.broadcast_to(jnp.repeat(s_dec, RET_DK)[:, None], (RET_HEADS * RET_DK, RET_DV))
    return dmask, q_dec, k_dec, s_dec


def _retention(p, s0, tables, *, n_seq, seq_len, n_outer, n_inner, table_by_inner):
    rows = n_seq * seq_len
    cosf, sinf, dmask, q_dec, k_dec, s_dec = tables
    t = p.shape[0]
    width = RET_HEADS * RET_DK
    rb = lambda o, c: o * n_inner + c
    tb = (lambda o, c: (c, 0)) if table_by_inner else (lambda o, c: (0, 0))
    const2 = lambda o, c: (0, 0)
    body = functools.partial(_ret_body, n_seq=n_seq, seq_len=seq_len)
    return pl.pallas_call(
        body, grid=(n_outer, n_inner),
        in_specs=[pl.BlockSpec((rows, width), lambda o, c: (rb(o, c), _COL['ret_q'] // 2)),
                  pl.BlockSpec((rows, width), lambda o, c: (rb(o, c), _COL['ret_k'] // 2)),
                  pl.BlockSpec((rows, GROUP_W), lambda o, c: (rb(o, c), _COL['ret_v'] // 4)),
                  pl.BlockSpec((rows, GROUP_W), lambda o, c: (rb(o, c), _COL['ret_g'] // 4)),
                  pl.BlockSpec((rows, width), tb), pl.BlockSpec((rows, width), tb),
                  pl.BlockSpec((RET_HEADS, rows, rows), lambda o, c: (0, 0, 0)),
                  pl.BlockSpec((rows, GROUP_W), const2), pl.BlockSpec((rows, width), const2),
                  pl.BlockSpec((width, RET_DV), const2),
                  pl.BlockSpec((n_seq, width, RET_DV), lambda o, c: (o, 0, 0))],
        out_specs=[pl.BlockSpec((rows, GROUP_W), lambda o, c: (rb(o, c), 0)),
                   pl.BlockSpec((n_seq, width, RET_DV), lambda o, c: (o, 0, 0))],
        out_shape=[jax.ShapeDtypeStruct((t, GROUP_W), BF16),
                   jax.ShapeDtypeStruct(s0.shape, F32)],
        scratch_shapes=[pltpu.VMEM((n_seq, width, RET_DV), F32)],
        name="retention")(p, p, p, p, cosf, sinf, dmask, q_dec, k_dec, s_dec, s0)


def _sb_scores(z, vis, run, tri):
    lp = jnp.log1p(jnp.exp(-jnp.abs(z)))
    ls = jnp.minimum(z, 0.0) - lp
    lm = jnp.minimum(-z, 0.0) - lp
    l = jnp.where(vis, lm, 0.0)
    l_hi = l.astype(BF16)
    l_lo = (l - l_hi.astype(F32)).astype(BF16)
    after = run + jnp.dot(l_hi, tri, preferred_element_type=F32) + jnp.dot(l_lo, tri, preferred_element_type=F32)
    w = jnp.where(vis, jnp.exp(ls + after), 0.0)
    return w, run + jnp.sum(l, axis=1, keepdims=True)


def _strict_lower(n):
    r = lax.broadcasted_iota(jnp.int32, (n, n), 0)
    c = lax.broadcasted_iota(jnp.int32, (n, n), 1)
    return (r > c).astype(BF16)


def _sb_prompt_body(q_ref, k_ref, v_ref, g_ref, o_ref):
    i = pl.program_id(2)
    q = q_ref[...].astype(BF16)
    scale = SB_HD ** -0.5
    row = lax.broadcasted_iota(jnp.int32, (BLK, BLK), 0)
    col = lax.broadcasted_iota(jnp.int32, (BLK, BLK), 1)
    tri = (row > col).astype(BF16)

    def step(jj, carry):
        acc, run = carry
        j = i - jj
        start = pl.multiple_of(j * BLK, BLK)
        kj = k_ref[pl.ds(start, BLK), :].astype(BF16)
        vj = v_ref[pl.ds(start, BLK), :].astype(BF16)
        z = lax.dot_general(q, kj, NT, preferred_element_type=F32) * scale
        vis = (j < i) | (col < row)
        w, run = _sb_scores(z, vis, run, tri)
        acc = acc + jnp.dot(w.astype(BF16), vj, preferred_element_type=F32)
        return acc, run

    acc, _ = lax.fori_loop(0, i + 1, step, (jnp.zeros((BLK, SB_HD), F32), jnp.zeros((BLK, 1), F32)))
    o_ref[...] = (acc * _silu(g_ref[...])).astype(o_ref.dtype)


def _sb_prompt(p, n, seq):
    nq = seq // BLK
    t = p.shape[0]
    return pl.pallas_call(
        _sb_prompt_body, grid=(n, SB_HEADS, nq),
        in_specs=[pl.BlockSpec((BLK, SB_HD), lambda b, h, i: (b * nq + i, _COL['sb_q'] + h)),
                  pl.BlockSpec((seq, SB_HD), lambda b, h, i: (b, _COL['sb_k'] + h)),
                  pl.BlockSpec((seq, SB_HD), lambda b, h, i: (b, _COL['sb_v'] + h)),
                  pl.BlockSpec((BLK, SB_HD), lambda b, h, i: (b * nq + i, _COL['sb_g'] + h))],
        out_specs=pl.BlockSpec((BLK, SB_HD), lambda b, h, i: (b * nq + i, h)),
        out_shape=jax.ShapeDtypeStruct((t, GROUP_W), BF16), name="sb_prompt")(p, p, p, p)


def _sb_sample_body(pt_ref, q_ref, kn_ref, vn_ref, g_ref, page_ref, o_ref, acc_ref, run_ref, *, seq_pad, valid_len):
    j = pl.program_id(1)
    m = SB_HEADS * seq_pad
    scale = SB_HD ** -0.5
    lane = lax.broadcasted_iota(jnp.int32, (seq_pad, GROUP_W), 1)
    q = q_ref[...]
    qbd = jnp.concatenate([jnp.where((lane // SB_HD) == h, q, 0.0) for h in range(SB_HEADS)], axis=0).astype(BF16)
    tri = _strict_lower(BLK)

    def update(kblk, vblk, vis):
        z = lax.dot_general(qbd, kblk.astype(BF16), NT, preferred_element_type=F32) * scale
        w, run = _sb_scores(z, vis, run_ref[...], tri)
        acc_ref[...] += jnp.dot(w.astype(BF16), vblk.astype(BF16), preferred_element_type=F32)
        run_ref[...] = run

    @pl.when(j == 0)
    def _():
        acc_ref[...] = jnp.zeros_like(acc_ref)
        run_ref[...] = jnp.zeros_like(run_ref)
        pad = jnp.zeros((BLK - seq_pad, GROUP_W), F32)
        kblk = jnp.concatenate([kn_ref[...], pad], axis=0)
        vblk = jnp.concatenate([vn_ref[...], pad], axis=0)
        tq = lax.broadcasted_iota(jnp.int32, (m, BLK), 0) % seq_pad
        tk = lax.broadcasted_iota(jnp.int32, (m, BLK), 1)
        update(kblk, vblk, (tk < tq) & (tk < valid_len))

    @pl.when(j > 0)
    def _():
        update(page_ref[0, :, 0:GROUP_W], page_ref[0, :, GROUP_W:2 * GROUP_W], jnp.full((m, BLK), True))

    @pl.when(j == pl.num_programs(1) - 1)
    def _():
        acc = acc_ref[...]
        out = jnp.zeros((seq_pad, GROUP_W), F32)
        for h in range(SB_HEADS):
            out = out + jnp.where((lane // SB_HD) == h, acc[h * seq_pad:(h + 1) * seq_pad], 0.0)
        o_ref[...] = (out * _silu(g_ref[...])).astype(o_ref.dtype)


def _sb_sample(p, cache, page_table, n, seq_pad, valid_len):
    npages = page_table.shape[1]
    t = p.shape[0]
    body = functools.partial(_sb_sample_body, seq_pad=seq_pad, valid_len=valid_len)
    page_idx = lambda b, j, pt: (pt[b, jnp.minimum(npages - j, npages - 1)], 0, 0)
    grid_spec = pltpu.PrefetchScalarGridSpec(
        num_scalar_prefetch=1, grid=(n, npages + 1),
        in_specs=[pl.BlockSpec((seq_pad, GROUP_W), lambda b, j, pt: (b, _COL['sb_q'] // 4)),
                  pl.BlockSpec((seq_pad, GROUP_W), lambda b, j, pt: (b, _COL['sb_k'] // 4)),
                  pl.BlockSpec((seq_pad, GROUP_W), lambda b, j, pt: (b, _COL['sb_v'] // 4)),
                  pl.BlockSpec((seq_pad, GROUP_W), lambda b, j, pt: (b, _COL['sb_g'] // 4)),
                  pl.BlockSpec((1, BLK, 2 * GROUP_W), page_idx)],
        out_specs=pl.BlockSpec((seq_pad, GROUP_W), lambda b, j, pt: (b, 0)),
        scratch_shapes=[pltpu.VMEM((SB_HEADS * seq_pad, GROUP_W), F32), pltpu.VMEM((SB_HEADS * seq_pad, 1), F32)])
    return pl.pallas_call(body, grid_spec=grid_spec, out_shape=jax.ShapeDtypeStruct((t, GROUP_W), BF16),
                          name="sb_sample")(page_table, p, p, p, p, cache)


def _topk_select(score, vis, topk):
    rows, width = score.shape
    score = jnp.where(vis, score, -jnp.inf)
    score = jnp.where(score == 0.0, 0.0, score)
    u = pltpu.bitcast(score, jnp.int32)
    key = u ^ ((u >> 31) & 0x7fffffff)
    int_min = jnp.int32(INT_MIN)

    def search(b, thr):
        cand = thr | jnp.left_shift(jnp.int32(1), 31 - b)
        cnt = jnp.sum((key >= (cand ^ int_min)).astype(F32), axis=1, keepdims=True)
        return jnp.where(cnt >= topk, cand, thr)

    thr = lax.fori_loop(0, 32, search, jnp.zeros((rows, 1), jnp.int32)) ^ int_min
    gt = key > thr
    eq = key == thr
    need = topk - jnp.sum(gt.astype(F32), axis=1, keepdims=True)
    r = lax.broadcasted_iota(jnp.int32, (BLK, BLK), 0)
    c = lax.broadcasted_iota(jnp.int32, (BLK, BLK), 1)
    before = (r < c).astype(BF16)
    seen = jnp.zeros((rows, 1), F32)
    pieces = []
    for b in range(width // BLK):
        eq_b = eq[:, b * BLK:(b + 1) * BLK]
        eq_f = eq_b.astype(BF16)
        rank = seen + jnp.dot(eq_f, before, preferred_element_type=F32)
        pieces.append(gt[:, b * BLK:(b + 1) * BLK] | (eq_b & (rank < need)))
        seen = seen + jnp.sum(eq_f.astype(F32), axis=1, keepdims=True)
    return jnp.concatenate(pieces, axis=1) & vis


def _dsa_prompt_body(q_ref, iq_ref, iw_ref, g_ref, k_ref, v_ref, ik_ref, o_ref, *, seq, topk):
    i = pl.program_id(1)
    ikk = ik_ref[...].astype(BF16)
    lane = lax.broadcasted_iota(jnp.int32, (BLK, LANE), 1)
    iw = iw_ref[...] * (IDX_HEADS ** -0.5)
    score = jnp.zeros((BLK, seq), F32)
    for h in range(IDX_HEADS):
        pair = iq_ref[:, (h // 2) * LANE:(h // 2 + 1) * LANE]
        iqh = jnp.where((lane // IDX_HD) == (h % 2), pair, 0.0).astype(BF16)
        rel = jnp.maximum(lax.dot_general(iqh, ikk, NT, preferred_element_type=F32), 0.0) * (IDX_HD ** -0.5)
        score = score + rel * iw[:, h:h + 1]
    qpos = i * BLK + lax.broadcasted_iota(jnp.int32, (BLK, seq), 0)
    kpos = lax.broadcasted_iota(jnp.int32, (BLK, seq), 1)
    sel = _topk_select(score, kpos <= qpos, topk)
    kb = k_ref[...].astype(BF16)
    vb = v_ref[...].astype(BF16)
    g = g_ref[...]
    for h in range(DSA_HEADS):
        qh = q_ref[:, h * DSA_HD:(h + 1) * DSA_HD].astype(BF16)
        lg = lax.dot_general(qh, kb, NT, preferred_element_type=F32) * (DSA_HD ** -0.5)
        lg = jnp.where(sel, lg, -jnp.inf)
        e = jnp.exp(lg - jnp.max(lg, axis=1, keepdims=True))
        o = jnp.dot(e.astype(BF16), vb, preferred_element_type=F32) / jnp.sum(e, axis=1, keepdims=True)
        o_ref[:, h * DSA_HD:(h + 1) * DSA_HD] = (o * _silu(g[:, h * DSA_HD:(h + 1) * DSA_HD])).astype(o_ref.dtype)


def _dsa_prompt(p, n, seq):
    nq = seq // BLK
    t = p.shape[0]
    topk = min(DSA_TOPK_MAX, seq // 4)
    body = functools.partial(_dsa_prompt_body, seq=seq, topk=topk)
    rowblk = lambda b, i: b * nq + i
    return pl.pallas_call(
        body, grid=(n, nq),
        in_specs=[pl.BlockSpec((BLK, GROUP_W), lambda b, i: (rowblk(b, i), _COL['dsa_q'] // 4)),
                  pl.BlockSpec((BLK, GROUP_W), lambda b, i: (rowblk(b, i), _COL['idx_q'] // 4)),
                  pl.BlockSpec((BLK, LANE), lambda b, i: (rowblk(b, i), _COL['idx_w'])),
                  pl.BlockSpec((BLK, GROUP_W), lambda b, i: (rowblk(b, i), _COL['dsa_g'] // 4)),
                  pl.BlockSpec((seq, DSA_HD), lambda b, i: (b, _COL['dsa_k'])),
                  pl.BlockSpec((seq, DSA_HD), lambda b, i: (b, _COL['dsa_v'])),
                  pl.BlockSpec((seq, LANE), lambda b, i: (b, _COL['idx_k']))],
        out_specs=pl.BlockSpec((BLK, GROUP_W), lambda b, i: (rowblk(b, i), 0)),
        out_shape=jax.ShapeDtypeStruct((t, GROUP_W), BF16), name="dsa_prompt")(p, p, p, p, p, p, p)


def _dsa_sample_body(pt_ref, q_ref, iq_ref, iw_ref, g_ref, kn_ref, vn_ref, ikn_ref, *rest,
                     npages, seq_pad, valid_len, topk):
    pages = rest[:npages]
    o_ref = rest[npages]
    nblk = npages + 1
    iq = iq_ref[...]
    iqs = jnp.concatenate([iq[:, h * IDX_HD:(h + 1) * IDX_HD] for h in range(IDX_HEADS)], axis=0).astype(BF16)
    iw = iw_ref[...] * (IDX_HEADS ** -0.5)
    zpad = lambda x: jnp.concatenate([x, jnp.zeros((BLK - seq_pad, x.shape[1]), F32)], axis=0)
    kblk = lambda b: pages[b][0, :, 0:DSA_HD] if b < npages else zpad(kn_ref[...])
    vblk = lambda b: pages[b][0, :, DSA_HD:2 * DSA_HD] if b < npages else zpad(vn_ref[...])
    iblk = lambda b: pages[b][0, :, 2 * DSA_HD:DSA_CACHE_W] if b < npages else zpad(ikn_ref[:, 0:IDX_HD])
    pieces = []
    for b in range(nblk):
        s = lax.dot_general(iqs, iblk(b).astype(BF16), NT, preferred_element_type=F32)
        rel = jnp.maximum(s, 0.0) * (IDX_HD ** -0.5)
        sc = jnp.zeros((seq_pad, BLK), F32)
        for h in range(IDX_HEADS):
            sc = sc + rel[h * seq_pad:(h + 1) * seq_pad] * iw[:, h:h + 1]
        pieces.append(sc)
    score = jnp.concatenate(pieces, axis=1)
    tq = lax.broadcasted_iota(jnp.int32, (seq_pad, nblk * BLK), 0)
    kk = lax.broadcasted_iota(jnp.int32, (seq_pad, nblk * BLK), 1)
    tnew = kk - npages * BLK
    vis = (tnew < 0) | ((tnew <= tq) & (tnew < valid_len))
    sel = _topk_select(score, vis, topk)
    q = q_ref[...]
    q4 = jnp.concatenate([q[:, h * DSA_HD:(h + 1) * DSA_HD] for h in range(DSA_HEADS)], axis=0).astype(BF16)
    lgs = []
    for b in range(nblk):
        lg = lax.dot_general(q4, kblk(b).astype(BF16), NT, preferred_element_type=F32) * (DSA_HD ** -0.5)
        sb = sel[:, b * BLK:(b + 1) * BLK]
        lgs.append(jnp.where(jnp.concatenate([sb] * DSA_HEADS, axis=0), lg, -jnp.inf))
    mx = lgs[0].max(axis=1, keepdims=True)
    for lg in lgs[1:]:
        mx = jnp.maximum(mx, lg.max(axis=1, keepdims=True))
    den = jnp.zeros((DSA_HEADS * seq_pad, 1), F32)
    acc = jnp.zeros((DSA_HEADS * seq_pad, DSA_HD), F32)
    for b in range(nblk):
        e = jnp.exp(lgs[b] - mx)
        den = den + jnp.sum(e, axis=1, keepdims=True)
        acc = acc + jnp.dot(e.astype(BF16), vblk(b).astype(BF16), preferred_element_type=F32)
    o = acc / den
    o = jnp.concatenate([o[h * seq_pad:(h + 1) * seq_pad] for h in range(DSA_HEADS)], axis=1)
    o_ref[...] = (o * _silu(g_ref[...])).astype(o_ref.dtype)


def _dsa_sample(p, cache, page_table, n, seq_pad, valid_len):
    npages = page_table.shape[1]
    t = p.shape[0]
    lk = npages * BLK + valid_len
    topk = min(DSA_TOPK_MAX, lk // 4)
    body = functools.partial(_dsa_sample_body, npages=npages, seq_pad=seq_pad, valid_len=valid_len, topk=topk)
    rows = lambda c: pl.BlockSpec((seq_pad, GROUP_W), lambda b, pt: (b, c // 4))
    rows1 = lambda c: pl.BlockSpec((seq_pad, LANE), lambda b, pt: (b, c))
    page_specs = [pl.BlockSpec((1, BLK, DSA_CACHE_W), functools.partial(lambda b, pt, a: (pt[b, a], 0, 0), a=a))
                  for a in range(npages)]
    grid_spec = pltpu.PrefetchScalarGridSpec(
        num_scalar_prefetch=1, grid=(n,),
        in_specs=[rows(_COL['dsa_q']), rows(_COL['idx_q']), rows1(_COL['idx_w']), rows(_COL['dsa_g']),
                  rows1(_COL['dsa_k']), rows1(_COL['dsa_v']), rows1(_COL['idx_k'])] + page_specs,
        out_specs=pl.BlockSpec((seq_pad, GROUP_W), lambda b, pt: (b, 0)))
    return pl.pallas_call(body, grid_spec=grid_spec, out_shape=jax.ShapeDtypeStruct((t, GROUP_W), BF16),
                          name="dsa_sample")(page_table, p, p, p, p, p, p, p, *([cache] * npages))


def _head_sum(x, ones_bd):
    hi = x.astype(BF16)
    lo = (x - hi.astype(F32)).astype(BF16)
    return jnp.dot(hi, ones_bd, preferred_element_type=F32) + jnp.dot(lo, ones_bd, preferred_element_type=F32)


def _softplus(x):
    return jnp.maximum(x, 0.0) + jnp.log1p(jnp.exp(-jnp.abs(x)))


def _rw_prep_body(ps_ref, prev_ref, first_ref, mu_ref, w0_ref, w2_ref, a0_ref, a2_ref, kkw_ref, kaw_ref, ones_ref,
                  r_ref, w_ref, k_ref, v_ref, kk_ref, ka_ref, *, n_seq, seq_len, n_inner):
    rows = n_seq * seq_len
    ps = ps_ref[...]
    xprev = pltpu.roll(ps, 1, 0)
    row = lax.broadcasted_iota(jnp.int32, (rows, SHIFT_P), 0)
    if n_seq == 1:
        c = pl.program_id(0) % n_inner
        first = jnp.where(c == 0, first_ref[0], prev_ref[SUBLANE - 1:SUBLANE, :])
        xprev = jnp.where(row == 0, first, xprev)
    else:
        for s in range(n_seq):
            xprev = jnp.where(row == s * seq_len, first_ref[0, s:s + 1, :], xprev)
    xs = ps + (xprev - ps) * mu_ref[...]
    r = xs[:, 0:GROUP_W]
    k = xs[:, GROUP_W:2 * GROUP_W]
    v = xs[:, 2 * GROUP_W:3 * GROUP_W]
    wl = xs[:, 3 * GROUP_W:3 * GROUP_W + LANE]
    al = xs[:, 3 * GROUP_W + LANE:3 * GROUP_W + 2 * LANE]
    wlin = w0_ref[...] + jnp.dot(jnp.tanh(wl).astype(BF16), w2_ref[...], preferred_element_type=F32)
    w_raw = -_softplus(-wlin) - 0.5
    decay = jnp.exp(-jnp.exp(w_raw))
    a = jax.nn.sigmoid(a0_ref[...] + jnp.dot(al.astype(BF16), a2_ref[...], preferred_element_type=F32))
    kk = k * kkw_ref[...]
    k2 = k * (1.0 + (a - 1.0) * kaw_ref[...])
    nrm = jnp.sqrt(_head_sum(kk * kk, ones_ref[...]))
    kkn = kk / jnp.maximum(nrm, 1e-12)
    r_ref[...] = r
    w_ref[...] = decay
    k_ref[...] = k2
    v_ref[...] = v
    kk_ref[...] = kkn
    ka_ref[...] = kkn * a


def _rw_prep(p, first_prev, prm, *, n_seq, seq_len, n_outer, n_inner):
    rows = n_seq * seq_len
    t = p.shape[0]
    nb = t // rows
    sub_per_blk = rows // SUBLANE
    body = functools.partial(_rw_prep_body, n_seq=n_seq, seq_len=seq_len, n_inner=n_inner)
    c1 = lambda i: (0, 0)
    vec = pl.BlockSpec((1, GROUP_W), c1)
    out = pl.BlockSpec((rows, GROUP_W), lambda i: (i, 0))
    return pl.pallas_call(
        body, grid=(nb,),
        in_specs=[pl.BlockSpec((rows, SHIFT_P), lambda i: (i, _COL['rw_shift'] // 16)),
                  pl.BlockSpec((SUBLANE, SHIFT_P), lambda i: (jnp.maximum(i * sub_per_blk - 1, 0), _COL['rw_shift'] // 16)),
                  pl.BlockSpec((1, n_seq, SHIFT_P), lambda i: (i // n_inner, 0, 0)),
                  pl.BlockSpec((1, SHIFT_P), c1), vec, pl.BlockSpec((LANE, GROUP_W), c1), vec,
                  pl.BlockSpec((LANE, GROUP_W), c1), vec, vec, pl.BlockSpec((GROUP_W, GROUP_W), c1)],
        out_specs=[out] * 6,
        out_shape=[jax.ShapeDtypeStruct((t, GROUP_W), F32)] * 6,
        name="rw_prep")(p, p, first_prev, prm['mu'], prm['w0'], prm['w2'], prm['a0'], prm['a2'],
                        prm['kk'], prm['ka'], prm['ones_bd'])


def _rw_scan_body(w_ref, kk_ref, ka_ref, k_ref, r_ref, v_ref, s0_ref, y_ref, sout_ref, s_ref, *, tb, nj, fold):
    @pl.when(pl.program_id(1) == 0)
    def _():
        s_ref[...] = s0_ref[...]

    si = s_ref.shape[1]
    unroll = 8

    def fold_lanes(x):
        return x + pltpu.roll(x, LANE // 2, 1) if fold else x

    def token(t, carry):
        def p1(j, sa):
            return sa + s_ref[j] * kk_ref[t, pl.ds(j, 1), :]
        sa = -fold_lanes(lax.fori_loop(0, nj, p1, jnp.zeros((si, LANE), F32), unroll=unroll))
        vt = v_ref[t]

        def p2(j, y):
            s_new = (s_ref[j] * w_ref[t, pl.ds(j, 1), :] + sa * ka_ref[t, pl.ds(j, 1), :]
                     + vt * k_ref[t, pl.ds(j, 1), :])
            s_ref[j] = s_new
            return y + s_new * r_ref[t, pl.ds(j, 1), :]
        y_ref[t] = fold_lanes(lax.fori_loop(0, nj, p2, jnp.zeros((si, LANE), F32), unroll=unroll))
        return carry

    lax.fori_loop(0, tb, token, 0)
    sout_ref[...] = s_ref[...]


def _rw_scan(rowvecs, v, s0, *, tb, fold):
    seq, nj, lanes = rowvecs[0].shape
    si = v.shape[1]
    ngroups = lanes // LANE
    body = functools.partial(_rw_scan_body, tb=tb, nj=nj, fold=fold)
    rv = pl.BlockSpec((tb, nj, LANE), lambda g, t: (t, 0, g))
    cv = pl.BlockSpec((tb, si, LANE), lambda g, t: (t, 0, g))
    st = pl.BlockSpec((nj, si, LANE), lambda g, t: (0, 0, g))
    return pl.pallas_call(
        body, grid=(ngroups, seq // tb),
        in_specs=[rv] * 5 + [cv, st],
        out_specs=[cv, st],
        out_shape=[jax.ShapeDtypeStruct((seq, si, lanes), F32), jax.ShapeDtypeStruct((nj, si, lanes), F32)],
        scratch_shapes=[pltpu.VMEM((nj, si, LANE), F32)],
        name="rw_scan")(*rowvecs, v, s0)


def _rw_post_body(y_ref, r_ref, k_ref, v_ref, g_ref, rk_ref, lnw_ref, lnb_ref, ones_ref, o_ref):
    ones_bd = ones_ref[...]
    y = y_ref[...]
    mu = _head_sum(y, ones_bd) * (1.0 / RW_HD)
    d = y - mu
    var = _head_sum(d * d, ones_bd) * (1.0 / RW_HD)
    yn = d * lax.rsqrt(var + RW_LN_EPS) * lnw_ref[...] + lnb_ref[...]
    v = v_ref[...]
    out = yn + _head_sum(r_ref[...] * k_ref[...] * rk_ref[...], ones_bd) * v
    o_ref[...] = (out * _silu(g_ref[...])).astype(o_ref.dtype)


def _rw_post(y, r, k, v, p, prm, rows=BLK):
    t = y.shape[0]
    blk = pl.BlockSpec((rows, GROUP_W), lambda i: (i, 0))
    vec = pl.BlockSpec((1, GROUP_W), lambda i: (0, 0))
    return pl.pallas_call(
        _rw_post_body, grid=(t // rows,),
        in_specs=[blk, blk, blk, blk, pl.BlockSpec((rows, GROUP_W), lambda i: (i, _COL['rw_g'] // 4)),
                  vec, vec, vec, pl.BlockSpec((GROUP_W, GROUP_W), lambda i: (0, 0))],
        out_specs=blk, out_shape=jax.ShapeDtypeStruct((t, GROUP_W), BF16),
        name="rw_post")(y, r, k, v, p, prm['rk'], prm['lnw'], prm['lnb'], prm['ones_bd'])


def _permute_shift(x):
    o1, o2, o3, o4 = GROUP_W, GROUP_W + RW_LORA, 2 * GROUP_W + RW_LORA, 3 * GROUP_W + RW_LORA
    z = lambda n: jnp.zeros(x.shape[:-1] + (n,), x.dtype)
    return jnp.concatenate([x[..., :o1], x[..., o2:o3], x[..., o3:o4], x[..., o1:o2], z(LANE - RW_LORA),
                            x[..., o4:], z(LANE - RW_LORA), z(2 * LANE)], axis=-1)


def _unpermute_shift(x):
    g = GROUP_W
    return jnp.concatenate([x[..., :g], x[..., 3 * g:3 * g + RW_LORA], x[..., g:2 * g], x[..., 2 * g:3 * g],
                            x[..., 3 * g + LANE:3 * g + LANE + RW_LORA]], axis=-1)


def _layout_w_in(w):
    d = w.shape[0]
    off = {}
    o = 0
    for name, n in (('ret', 3 * GROUP_W), ('sb', 4 * GROUP_W), ('dsa_q', GROUP_W), ('dsa_k', DSA_HD), ('dsa_v', DSA_HD),
                    ('idx_q', IDX_HEADS * IDX_HD), ('idx_k', IDX_HD), ('idx_w', IDX_HEADS), ('dsa_g', GROUP_W),
                    ('rw_shift', SHIFT_W), ('rw_g', GROUP_W)):
        off[name] = (o, o + n)
        o += n
    sl = lambda name: w[:, off[name][0]:off[name][1]]
    z = lambda n: jnp.zeros((d, n), w.dtype)
    parts = [sl('ret'), sl('sb'), sl('dsa_q'), sl('idx_q'), sl('dsa_g'), sl('dsa_k'), sl('dsa_v'),
             sl('idx_k'), sl('idx_k'), sl('idx_w'), z(LANE - IDX_HEADS), sl('rw_g'), _permute_shift(sl('rw_shift'))]
    out = jnp.concatenate(parts, axis=1).astype(BF16)
    assert out.shape[1] == N_COLS
    return out


def _pad_lora(w2):
    return jnp.concatenate([w2, jnp.zeros((LANE - RW_LORA, GROUP_W), w2.dtype)], axis=0).astype(BF16)


def _layer_params(l, norm_g, w_in, shift_mu, rw_w0, rw_w2, rw_a0, rw_a2, rw_kk, rw_ka, rw_rk, rw_lnx_w, rw_lnx_b, w_out):
    head = np.arange(GROUP_W) // RW_HD
    row = lambda x: x.reshape(1, -1)
    return dict(g=norm_g[l], w_in=_layout_w_in(w_in[l]), w_out=w_out[l].astype(BF16),
                mu=row(_permute_shift(shift_mu[l])), w0=row(rw_w0[l]), w2=_pad_lora(rw_w2[l]),
                a0=row(rw_a0[l]), a2=_pad_lora(rw_a2[l]), kk=row(rw_kk[l]), ka=row(rw_ka[l]),
                rk=row(rw_rk[l]), lnw=row(rw_lnx_w[l]), lnb=row(rw_lnx_b[l]),
                ones_bd=jnp.asarray(head[:, None] == head[None, :], BF16))


def _scan_prompt(prep, n, seq, s0_zero=True):
    r, w, k, v, kk, ka = prep
    half = RW_HD // 2

    def rowvec(x):
        return x.reshape(n, seq, RW_HEADS, 2, half).transpose(1, 4, 3, 0, 2).reshape(seq, half, 2 * n * RW_HEADS)

    def colvec(x):
        y = x.reshape(n, seq, RW_HEADS, RW_HD).transpose(1, 3, 0, 2).reshape(seq, RW_HD, n * RW_HEADS)
        return jnp.concatenate([y, y], axis=-1)

    s0 = jnp.zeros((half, RW_HD, 2 * n * RW_HEADS), F32)
    y, s = _rw_scan([rowvec(w), rowvec(kk), rowvec(ka), rowvec(k), rowvec(r)], colvec(v), s0, tb=32, fold=True)
    nh = n * RW_HEADS
    y = y[:, :, :nh].reshape(seq, RW_HD, n, RW_HEADS).transpose(2, 0, 3, 1).reshape(n * seq, GROUP_W)
    s = s.reshape(half, RW_HD, 2, n, RW_HEADS).transpose(3, 4, 1, 2, 0).reshape(n, RW_HEADS, RW_HD, RW_HD)
    return y, s


def _scan_sample(prep, s0, n, seq_pad, valid_len):
    r, w, k, v, kk, ka = prep

    def vec(x):
        return x.reshape(n, seq_pad, RW_HEADS, RW_HD)[:, :valid_len].transpose(1, 3, 0, 2).reshape(
            valid_len, RW_HD, n * RW_HEADS)

    s0 = s0.transpose(3, 2, 0, 1).reshape(RW_HD, RW_HD, n * RW_HEADS)
    y, s = _rw_scan([vec(w), vec(kk), vec(ka), vec(k), vec(r)], vec(v), s0, tb=valid_len, fold=False)
    y = y.reshape(valid_len, RW_HD, n, RW_HEADS).transpose(2, 0, 3, 1)
    y = jnp.pad(y, ((0, 0), (0, seq_pad - valid_len), (0, 0), (0, 0))).reshape(n * seq_pad, GROUP_W)
    s = s.reshape(RW_HD, RW_HD, n, RW_HEADS).transpose(2, 3, 1, 0)
    return y, s


def _prompt_layer(x, h, prm, n, seq, tables, last_g, last_dtype):
    nq = seq // BLK
    p = _in_proj(h, prm['w_in'])
    ret_o, ret_s = _retention(p, jnp.zeros((n, RET_HEADS * RET_DK, RET_DV), F32), tables,
                              n_seq=1, seq_len=BLK, n_outer=n, n_inner=nq, table_by_inner=True)
    sb_o = _sb_prompt(p, n, seq)
    dsa_o = _dsa_prompt(p, n, seq)
    prep = _rw_prep(p, jnp.zeros((n, 1, SHIFT_P), F32), prm, n_seq=1, seq_len=BLK, n_outer=n, n_inner=nq)
    y, wkv_s = _scan_prompt(prep, n, seq)
    rw_o = _rw_post(y, prep[0], prep[2], prep[3], p, prm)
    mix = jnp.concatenate([ret_o, sb_o, dsa_o, rw_o], axis=1)
    x, h = _out_proj(x, mix, prm['w_out'], last_g, last_dtype)
    p3 = p.reshape(n, seq, N_COLS)
    c = lambda name: _COL[name] * LANE
    sb_new = p3[:, :, c('sb_k'):c('sb_g')].reshape(n, seq, 2, SB_HEADS, SB_HD)
    dsa_new = p3[:, :, c('dsa_k'):c('dsa_k') + DSA_CACHE_W]
    shift = _unpermute_shift(p3[:, seq - 1, c('rw_shift'):])
    return x, h, (sb_new, dsa_new, ret_s.reshape(n, RET_HEADS, RET_DK, RET_DV), wkv_s, shift)


def _sample_layer(x, h, prm, n, seq_pad, valid_len, tables, past, page_table, last_g, last_dtype):
    shift_prev, ret_s0, wkv_s0, sb_cache, dsa_cache = past
    n_seq = BLK // seq_pad
    nb = n // n_seq
    p = _in_proj(h, prm['w_in'])
    ret_o, ret_s = _retention(p, ret_s0.reshape(n, RET_HEADS * RET_DK, RET_DV), tables,
                              n_seq=n_seq, seq_len=seq_pad, n_outer=nb, n_inner=1, table_by_inner=False)
    sb_o = _sb_sample(p, sb_cache, page_table, n, seq_pad, valid_len)
    dsa_o = _dsa_sample(p, dsa_cache, page_table, n, seq_pad, valid_len)
    first_prev = _permute_shift(shift_prev).reshape(nb, n_seq, SHIFT_P)
    prep = _rw_prep(p, first_prev, prm, n_seq=n_seq, seq_len=seq_pad, n_outer=nb, n_inner=1)
    y, wkv_s = _scan_sample(prep, wkv_s0, n, seq_pad, valid_len)
    rw_o = _rw_post(y, prep[0], prep[2], prep[3], p, prm)
    mix = jnp.concatenate([ret_o, sb_o, dsa_o, rw_o], axis=1)
    x, h = _out_proj(x, mix, prm['w_out'], last_g, last_dtype)
    p3 = p.reshape(n, seq_pad, N_COLS)[:, :valid_len]
    c = lambda name: _COL[name] * LANE
    sb_new = p3[:, :, c('sb_k'):c('sb_g')].reshape(n, valid_len, 2, SB_HEADS, SB_HD)
    dsa_new = p3[:, :, c('dsa_k'):c('dsa_k') + DSA_CACHE_W]
    shift = _unpermute_shift(p3[:, valid_len - 1, c('rw_shift'):])
    return x, h, (sb_new, dsa_new, ret_s.reshape(n, RET_HEADS, RET_DK, RET_DV), wkv_s, shift)


def kernel(x_prompt, x_sample, cache_sb_kv, cache_dsa, state_ret, state_wkv, state_shift, page_table,
           norm_g, w_in, shift_mu, rw_w0, rw_w2, rw_a0, rw_a2, rw_kk, rw_ka, rw_rk, rw_lnx_w, rw_lnx_b,
           w_out, final_g):
    n_p, seq, d = x_prompt.shape
    n_s, dec_seq, _ = x_sample.shape
    depth = norm_g.shape[0]
    past_len = page_table.shape[1] * cache_sb_kv.shape[2]
    assert cache_sb_kv.shape[2] == BLK and seq % BLK == 0 and d == 4 * GROUP_W
    seq_pad = SUBLANE
    assert dec_seq <= seq_pad and (n_s * seq_pad) % BLK == 0
    n_seq = BLK // seq_pad

    prms = [_layer_params(l, norm_g, w_in, shift_mu, rw_w0, rw_w2, rw_a0, rw_a2, rw_kk, rw_ka, rw_rk,
                          rw_lnx_w, rw_lnx_b, w_out) for l in range(depth)]
    pos = np.arange(seq)
    rows = np.arange(BLK)
    tables_p = _rotary_tables(pos) + _decay_tables(rows, np.zeros(BLK, np.int64), float(BLK))
    tables_s = (_rotary_tables(past_len + rows % seq_pad)
                + _decay_tables(rows % seq_pad, rows // seq_pad, float(dec_seq)))

    n_pool = cache_sb_kv.shape[1]
    sb_pages = cache_sb_kv.reshape(depth * n_pool, BLK, 2 * GROUP_W)
    dsa_pages = cache_dsa.reshape(depth * n_pool, BLK, DSA_CACHE_W)
    xp = x_prompt.reshape(n_p * seq, d)
    xs = jnp.pad(x_sample, ((0, 0), (0, seq_pad - dec_seq), (0, 0))).reshape(n_s * seq_pad, d)
    hp = _rmsnorm(xp, prms[0]['g'], BF16)
    hs = _rmsnorm(xs, prms[0]['g'], BF16)
    outs_p, outs_s = [], []
    for l in range(depth):
        last = l == depth - 1
        g_next = final_g if last else prms[l + 1]['g']
        dt = F32 if last else BF16
        xp, hp, st = _prompt_layer(xp, hp, prms[l], n_p, seq, tables_p, g_next, dt)
        outs_p.append(st)
        past = (state_shift[l], state_ret[l], state_wkv[l], sb_pages, dsa_pages)
        xs, hs, st = _sample_layer(xs, hs, prms[l], n_s, seq_pad, dec_seq, tables_s, past,
                                   page_table + l * n_pool, g_next, dt)
        outs_s.append(st)
    y_prompt = hp.reshape(n_p, seq, d)
    y_sample = hs.reshape(n_s, seq_pad, d)[:, :dec_seq]
    stack = lambda outs, i: jnp.stack([o[i] for o in outs])
    return (y_prompt, y_sample, *[stack(outs_p, i) for i in range(5)], *[stack(outs_s, i) for i in range(5)])
```

```python
import functools
import math

import jax
import jax.numpy as jnp
import numpy as np
from jax import lax
from jax.experimental import pallas as pl
from jax.experimental.pallas import tpu as pltpu

F32 = jnp.float32
BF16 = jnp.bfloat16
LANE = 128
SUBLANE = 8

GROUP_W = 512
RET_HEADS, RET_DK, RET_DV = 4, 64, 128
SB_HEADS, SB_HD = 4, 128
DSA_HEADS, DSA_HD = 4, 128
IDX_HEADS, IDX_HD = 8, 64
DSA_TOPK_MAX = 256
DSA_CACHE_W = 2 * DSA_HD + IDX_HD
RW_HD, RW_HEADS = 64, 8
RW_LORA = 96
SHIFT_W = 3 * GROUP_W + 2 * RW_LORA
RW_LN_EPS = 64e-5
NORM_EPS = 1e-6
ROPE_BASE = 10000.0
BLK = 128

_COL = dict(ret_q=0, ret_k=2, ret_v=4, ret_g=8, sb_q=12, sb_k=16, sb_v=20, sb_g=24,
            dsa_q=28, idx_q=32, dsa_g=36, dsa_k=40, dsa_v=41, idx_k=42, idx_w=43,
            rw_g=44, rw_shift=48)
N_COLS = 64 * LANE
SHIFT_P = 16 * LANE
NT = (((1,), (1,)), ((), ()))
TN = (((0,), (0,)), ((), ()))
INT_MIN = -2 ** 31


def _silu(x):
    return x * jax.nn.sigmoid(x)


def _rmsnorm_body(x_ref, g_ref, o_ref):
    x = x_ref[...]
    y = x * lax.rsqrt(jnp.mean(x * x, axis=-1, keepdims=True) + NORM_EPS)
    o_ref[...] = (y * g_ref[...]).astype(o_ref.dtype)


def _rmsnorm(x, g, out_dtype, tm=256):
    t, d = x.shape
    tm = min(tm, t)
    return pl.pallas_call(
        _rmsnorm_body, grid=(t // tm,),
        in_specs=[pl.BlockSpec((tm, d), lambda i: (i, 0)), pl.BlockSpec((1, d), lambda i: (0, 0))],
        out_specs=pl.BlockSpec((tm, d), lambda i: (i, 0)),
        out_shape=jax.ShapeDtypeStruct((t, d), out_dtype), name="rmsnorm")(x, g.reshape(1, d))


def _matmul_body(a_ref, b_ref, o_ref):
    o_ref[...] = jnp.dot(a_ref[...], b_ref[...], preferred_element_type=F32)


def _in_proj(h, w, tm=512, tn=1024):
    t, d = h.shape
    nc = w.shape[1]
    tm = min(tm, t)
    return pl.pallas_call(
        _matmul_body, grid=(t // tm, nc // tn),
        in_specs=[pl.BlockSpec((tm, d), lambda i, j: (i, 0)), pl.BlockSpec((d, tn), lambda i, j: (0, j))],
        out_specs=pl.BlockSpec((tm, tn), lambda i, j: (i, j)),
        out_shape=jax.ShapeDtypeStruct((t, nc), F32), name="in_proj")(h, w)


def _out_proj_body(x_ref, m_ref, w_ref, g_ref, xo_ref, yo_ref):
    xn = x_ref[...] + jnp.dot(m_ref[...], w_ref[...], preferred_element_type=F32)
    xo_ref[...] = xn
    y = xn * lax.rsqrt(jnp.mean(xn * xn, axis=-1, keepdims=True) + NORM_EPS)
    yo_ref[...] = (y * g_ref[...]).astype(yo_ref.dtype)


def _out_proj(x, mix, w, g, y_dtype, tm=256):
    t, d = x.shape
    tm = min(tm, t)
    return pl.pallas_call(
        _out_proj_body, grid=(t // tm,),
        in_specs=[pl.BlockSpec((tm, d), lambda i: (i, 0)), pl.BlockSpec((tm, d), lambda i: (i, 0)),
                  pl.BlockSpec((d, d), lambda i: (0, 0)), pl.BlockSpec((1, d), lambda i: (0, 0))],
        out_specs=[pl.BlockSpec((tm, d), lambda i: (i, 0)), pl.BlockSpec((tm, d), lambda i: (i, 0))],
        out_shape=[jax.ShapeDtypeStruct((t, d), F32), jax.ShapeDtypeStruct((t, d), y_dtype)],
        name="out_proj")(x, mix, w, g.reshape(1, d))


def _ret_body(q_ref, k_ref, v_ref, g_ref, cos_ref, sin_ref, dmask_ref, qdec_ref, kdec_ref, sdec_ref,
              s0_ref, o_ref, sout_ref, ss_ref, *, n_seq, seq_len):
    rows = n_seq * seq_len

    @pl.when(pl.program_id(1) == 0)
    def _():
        ss_ref[...] = s0_ref[...]

    lane = lax.broadcasted_iota(jnp.int32, (rows, RET_HEADS * RET_DK), 1)
    first_half = (lane % RET_DK) < (RET_DK // 2)
    cosf, sinf = cos_ref[...], sin_ref[...]
    width = RET_HEADS * RET_DK

    def rot(x):
        swapped = jnp.where(first_half, pltpu.roll(x, width - RET_DK // 2, 1), pltpu.roll(x, RET_DK // 2, 1))
        return x * cosf + swapped * sinf

    qr = rot(q_ref[...])
    kr = rot(k_ref[...]) * (RET_DK ** -0.5)
    kr_b = kr.astype(BF16)
    kd = kr * kdec_ref[...]
    v = v_ref[...]
    g = g_ref[...]
    qdec = qdec_ref[...]
    upd = [jnp.zeros((width, RET_DV), F32) for _ in range(n_seq)]
    for h in range(RET_HEADS):
        mh = (lane // RET_DK) == h
        qh = jnp.where(mh, qr, 0.0).astype(BF16)
        att = lax.dot_general(qh, kr_b, NT, preferred_element_type=F32) * dmask_ref[h]
        vh = v[:, h * RET_DV:(h + 1) * RET_DV].astype(BF16)
        o = jnp.dot(att.astype(BF16), vh, preferred_element_type=F32)
        inter = [jnp.dot(qh[s * seq_len:(s + 1) * seq_len], ss_ref[s].astype(BF16), preferred_element_type=F32)
                 for s in range(n_seq)]
        inter = inter[0] if n_seq == 1 else jnp.concatenate(inter, axis=0)
        o = o + inter * qdec[:, h * RET_DV:(h + 1) * RET_DV]
        o = o * lax.rsqrt(jnp.mean(o * o, axis=-1, keepdims=True) + NORM_EPS)
        o_ref[:, h * RET_DV:(h + 1) * RET_DV] = (o * _silu(g[:, h * RET_DV:(h + 1) * RET_DV])).astype(o_ref.dtype)
        kdh = jnp.where(mh, kd, 0.0).astype(BF16)
        for s in range(n_seq):
            upd[s] = upd[s] + lax.dot_general(kdh[s * seq_len:(s + 1) * seq_len], vh[s * seq_len:(s + 1) * seq_len],
                                              TN, preferred_element_type=F32)
    sdec = sdec_ref[...]
    for s in range(n_seq):
        new = ss_ref[s] * sdec + upd[s]
        ss_ref[s] = new
        sout_ref[s] = new


def _rotary_tables(pos_abs):
    half = RET_DK // 2
    inv = 1.0 / (ROPE_BASE ** (jnp.arange(half, dtype=F32) / half))
    ang = jnp.asarray(pos_abs, F32)[:, None] * inv[None, :]
    cos, sin = jnp.cos(ang), jnp.sin(ang)
    cosf = jnp.tile(jnp.concatenate([cos, cos], axis=1), (1, RET_HEADS))
    sinf = jnp.tile(jnp.concatenate([-sin, sin], axis=1), (1, RET_HEADS))
    return cosf, sinf


def _decay_tables(pos_in_seq, seq_id, valid_len):
    log_g = jnp.log1p(-jnp.exp2(-5.0 - jnp.arange(RET_HEADS, dtype=F32)))
    i = jnp.asarray(pos_in_seq, F32)
    sid = jnp.asarray(seq_id)
    diff = i[:, None] - i[None, :]
    same = sid[:, None] == sid[None, :]
    dmask = jnp.where((diff >= 0)[None] & same[None],
                      jnp.exp(jnp.maximum(diff, 0.0)[None] * log_g[:, None, None]), 0.0)
    q_dec = jnp.exp((i + 1.0)[:, None] * log_g[None, :])
    k_dec = jnp.where((i < valid_len)[:, None], jnp.exp((valid_len - 1.0 - i)[:, None] * log_g[None, :]), 0.0)
    s_dec = jnp.exp(valid_len * log_g)
    q_dec = jnp.repeat(q_dec, RET_DV, axis=1)
    k_dec = jnp.repeat(k_dec, RET_DK, axis=1)
    s_dec = jnp.broadcast_to(jnp.repeat(s_dec, RET_DK)[:, None], (RET_HEADS * RET_DK, RET_DV))
    return dmask, q_dec, k_dec, s_dec


def _retention(p, s0, tables, *, n_seq, seq_len, n_outer, n_inner, table_by_inner):
    rows = n_seq * seq_len
    cosf, sinf, dmask, q_dec, k_dec, s_dec = tables
    t = p.shape[0]
    width = RET_HEADS * RET_DK
    rb = lambda o, c: o * n_inner + c
    tb = (lambda o, c: (c, 0)) if table_by_inner else (lambda o, c: (0, 0))
    const2 = lambda o, c: (0, 0)
    body = functools.partial(_ret_body, n_seq=n_seq, seq_len=seq_len)
    return pl.pallas_call(
        body, grid=(n_outer, n_inner),
        in_specs=[pl.BlockSpec((rows, width), lambda o, c: (rb(o, c), _COL['ret_q'] // 2)),
                  pl.BlockSpec((rows, width), lambda o, c: (rb(o, c), _COL['ret_k'] // 2)),
                  pl.BlockSpec((rows, GROUP_W), lambda o, c: (rb(o, c), _COL['ret_v'] // 4)),
                  pl.BlockSpec((rows, GROUP_W), lambda o, c: (rb(o, c), _COL['ret_g'] // 4)),
                  pl.BlockSpec((rows, width), tb), pl.BlockSpec((rows, width), tb),
                  pl.BlockSpec((RET_HEADS, rows, rows), lambda o, c: (0, 0, 0)),
                  pl.BlockSpec((rows, GROUP_W), const2), pl.BlockSpec((rows, width), const2),
                  pl.BlockSpec((width, RET_DV), const2),
                  pl.BlockSpec((n_seq, width, RET_DV), lambda o, c: (o, 0, 0))],
        out_specs=[pl.BlockSpec((rows, GROUP_W), lambda o, c: (rb(o, c), 0)),
                   pl.BlockSpec((n_seq, width, RET_DV), lambda o, c: (o, 0, 0))],
        out_shape=[jax.ShapeDtypeStruct((t, GROUP_W), BF16),
                   jax.ShapeDtypeStruct(s0.shape, F32)],
        scratch_shapes=[pltpu.VMEM((n_seq, width, RET_DV), F32)],
        name="retention")(p, p, p, p, cosf, sinf, dmask, q_dec, k_dec, s_dec, s0)


def _sb_scores(z, vis, run, tri):
    lp = jnp.log1p(jnp.exp(-jnp.abs(z)))
    ls = jnp.minimum(z, 0.0) - lp
    lm = jnp.minimum(-z, 0.0) - lp
    l = jnp.where(vis, lm, 0.0)
    l_hi = l.astype(BF16)
    l_lo = (l - l_hi.astype(F32)).astype(BF16)
    after = run + jnp.dot(l_hi, tri, preferred_element_type=F32) + jnp.dot(l_lo, tri, preferred_element_type=F32)
    w = jnp.where(vis, jnp.exp(ls + after), 0.0)
    return w, run + jnp.sum(l, axis=1, keepdims=True)


SB_DEAD = -104.0


def _sb_live(run):
    return (jnp.max(run) >= SB_DEAD).astype(jnp.int32)


def _strict_lower(n):
    r = lax.broadcasted_iota(jnp.int32, (n, n), 0)
    c = lax.broadcasted_iota(jnp.int32, (n, n), 1)
    return (r > c).astype(BF16)


def _sb_prompt_body(q_ref, k_ref, v_ref, g_ref, o_ref):
    i = pl.program_id(2)
    q = q_ref[...].astype(BF16)
    scale = SB_HD ** -0.5
    row = lax.broadcasted_iota(jnp.int32, (BLK, BLK), 0)
    col = lax.broadcasted_iota(jnp.int32, (BLK, BLK), 1)
    tri = (row > col).astype(BF16)

    def cond(carry):
        jj, _, _, live = carry
        return (jj <= i) & (live > 0)

    def step(carry):
        jj, acc, run, _ = carry
        j = i - jj
        start = pl.multiple_of(j * BLK, BLK)
        kj = k_ref[pl.ds(start, BLK), :].astype(BF16)
        vj = v_ref[pl.ds(start, BLK), :].astype(BF16)
        z = lax.dot_general(q, kj, NT, preferred_element_type=F32) * scale
        vis = (j < i) | (col < row)
        w, run = _sb_scores(z, vis, run, tri)
        acc = acc + jnp.dot(w.astype(BF16), vj, preferred_element_type=F32)
        return jj + 1, acc, run, _sb_live(run)

    init = (jnp.int32(0), jnp.zeros((BLK, SB_HD), F32), jnp.zeros((BLK, 1), F32), jnp.int32(1))
    _, acc, _, _ = lax.while_loop(cond, step, init)
    o_ref[...] = (acc * _silu(g_ref[...])).astype(o_ref.dtype)


def _sb_prompt(p, n, seq):
    nq = seq // BLK
    t = p.shape[0]
    return pl.pallas_call(
        _sb_prompt_body, grid=(n, SB_HEADS, nq),
        in_specs=[pl.BlockSpec((BLK, SB_HD), lambda b, h, i: (b * nq + i, _COL['sb_q'] + h)),
                  pl.BlockSpec((seq, SB_HD), lambda b, h, i: (b, _COL['sb_k'] + h)),
                  pl.BlockSpec((seq, SB_HD), lambda b, h, i: (b, _COL['sb_v'] + h)),
                  pl.BlockSpec((BLK, SB_HD), lambda b, h, i: (b * nq + i, _COL['sb_g'] + h))],
        out_specs=pl.BlockSpec((BLK, SB_HD), lambda b, h, i: (b * nq + i, h)),
        out_shape=jax.ShapeDtypeStruct((t, GROUP_W), BF16), name="sb_prompt")(p, p, p, p)


SB_PAGE_ROWS = BLK * 2 * SB_HEADS


def _sb_sample_body(pt_ref, q_ref, kn_ref, vn_ref, g_ref, cache_ref, o_ref, buf_ref, sem_ref, *,
                    seq_pad, valid_len, npages):
    b = pl.program_id(0)
    m = SB_HEADS * seq_pad
    scale = SB_HD ** -0.5
    tri = _strict_lower(BLK)
    q = q_ref[...]
    heads = range(SB_HEADS)
    head_cols = lambda x, h: x[:, h * SB_HD:(h + 1) * SB_HD]
    qh = [head_cols(q, h).astype(BF16) for h in heads]

    def page_copy(j, slot):
        page = pt_ref[b, npages - 1 - j]
        src = cache_ref.at[pl.ds(pl.multiple_of(page * SB_PAGE_ROWS, SB_PAGE_ROWS), SB_PAGE_ROWS), :]
        return pltpu.make_async_copy(src, buf_ref.at[slot], sem_ref.at[slot])

    page_copy(0, 0).start()

    def attend(kh, vh, vis, acc, run):
        z = jnp.concatenate([lax.dot_general(qh[h], kh[h].astype(BF16), NT, preferred_element_type=F32)
                             for h in heads], axis=0) * scale
        w, run = _sb_scores(z, vis, run, tri)
        w = w.astype(BF16)
        upd = jnp.concatenate([jnp.dot(w[h * seq_pad:(h + 1) * seq_pad], vh[h].astype(BF16),
                                       preferred_element_type=F32) for h in heads], axis=0)
        return acc + upd, run

    pad = jnp.zeros((BLK - seq_pad, GROUP_W), F32)
    kn = jnp.concatenate([kn_ref[...], pad], axis=0)
    vn = jnp.concatenate([vn_ref[...], pad], axis=0)
    tq = lax.broadcasted_iota(jnp.int32, (m, BLK), 0) % seq_pad
    tk = lax.broadcasted_iota(jnp.int32, (m, BLK), 1)
    acc, run = attend([head_cols(kn, h) for h in heads], [head_cols(vn, h) for h in heads],
                      (tk < tq) & (tk < valid_len), jnp.zeros((m, SB_HD), F32), jnp.zeros((m, 1), F32))
    all_visible = jnp.full((m, BLK), True)

    def cond(carry):
        j, _, _, live = carry
        return (j < npages) & (live > 0)

    def step(carry):
        j, acc, run, _ = carry
        slot = j % 2
        page_copy(j, slot).wait()

        @pl.when(j + 1 < npages)
        def _():
            page_copy(j + 1, 1 - slot).start()

        kh = [buf_ref[slot, pl.ds(h, BLK, stride=2 * SB_HEADS), :] for h in heads]
        vh = [buf_ref[slot, pl.ds(SB_HEADS + h, BLK, stride=2 * SB_HEADS), :] for h in heads]
        acc, run = attend(kh, vh, all_visible, acc, run)
        return j + 1, acc, run, _sb_live(run)

    j, acc, _, _ = lax.while_loop(cond, step, (jnp.int32(0), acc, run, _sb_live(run)))

    @pl.when(j < npages)
    def _():
        page_copy(j, j % 2).wait()

    out = jnp.concatenate([acc[h * seq_pad:(h + 1) * seq_pad] for h in heads], axis=1)
    o_ref[...] = (out * _silu(g_ref[...])).astype(o_ref.dtype)


def _sb_sample(p, cache, page_table, n, seq_pad, valid_len):
    npages = page_table.shape[1]
    t = p.shape[0]
    body = functools.partial(_sb_sample_body, seq_pad=seq_pad, valid_len=valid_len, npages=npages)
    rows = lambda c: pl.BlockSpec((seq_pad, GROUP_W), lambda b, pt: (b, c // 4))
    grid_spec = pltpu.PrefetchScalarGridSpec(
        num_scalar_prefetch=1, grid=(n,),
        in_specs=[rows(_COL['sb_q']), rows(_COL['sb_k']), rows(_COL['sb_v']), rows(_COL['sb_g']),
                  pl.BlockSpec(memory_space=pl.ANY)],
        out_specs=pl.BlockSpec((seq_pad, GROUP_W), lambda b, pt: (b, 0)),
        scratch_shapes=[pltpu.VMEM((2, SB_PAGE_ROWS, SB_HD), F32), pltpu.SemaphoreType.DMA((2,))])
    return pl.pallas_call(body, grid_spec=grid_spec, out_shape=jax.ShapeDtypeStruct((t, GROUP_W), BF16),
                          name="sb_sample")(page_table, p, p, p, p, cache)


def _topk_select(score, vis, topk):
    rows, width = score.shape
    score = jnp.where(vis, score, -jnp.inf)
    score = jnp.where(score == 0.0, 0.0, score)
    u = pltpu.bitcast(score, jnp.int32)
    key = u ^ ((u >> 31) & 0x7fffffff)
    int_min = jnp.int32(INT_MIN)

    def search(b, thr):
        cand = thr | jnp.left_shift(jnp.int32(1), 31 - b)
        cnt = jnp.sum((key >= (cand ^ int_min)).astype(F32), axis=1, keepdims=True)
        return jnp.where(cnt >= topk, cand, thr)

    thr = lax.fori_loop(0, 32, search, jnp.zeros((rows, 1), jnp.int32)) ^ int_min
    gt = key > thr
    eq = key == thr
    need = topk - jnp.sum(gt.astype(F32), axis=1, keepdims=True)
    r = lax.broadcasted_iota(jnp.int32, (BLK, BLK), 0)
    c = lax.broadcasted_iota(jnp.int32, (BLK, BLK), 1)
    before = (r < c).astype(BF16)
    seen = jnp.zeros((rows, 1), F32)
    pieces = []
    for b in range(width // BLK):
        eq_b = eq[:, b * BLK:(b + 1) * BLK]
        eq_f = eq_b.astype(BF16)
        rank = seen + jnp.dot(eq_f, before, preferred_element_type=F32)
        pieces.append(gt[:, b * BLK:(b + 1) * BLK] | (eq_b & (rank < need)))
        seen = seen + jnp.sum(eq_f.astype(F32), axis=1, keepdims=True)
    return jnp.concatenate(pieces, axis=1) & vis


def _dsa_prompt_body(q_ref, iq_ref, iw_ref, g_ref, k_ref, v_ref, ik_ref, o_ref, *, topk, widths):
    i = pl.program_id(1)

    def compute(width):
        ikk = ik_ref[0:width, :].astype(BF16)
        lane = lax.broadcasted_iota(jnp.int32, (BLK, LANE), 1)
        iw = iw_ref[...] * (IDX_HEADS ** -0.5)
        score = jnp.zeros((BLK, width), F32)
        for h in range(IDX_HEADS):
            pair = iq_ref[:, (h // 2) * LANE:(h // 2 + 1) * LANE]
            iqh = jnp.where((lane // IDX_HD) == (h % 2), pair, 0.0).astype(BF16)
            rel = jnp.maximum(lax.dot_general(iqh, ikk, NT, preferred_element_type=F32), 0.0) * (IDX_HD ** -0.5)
            score = score + rel * iw[:, h:h + 1]
        qpos = i * BLK + lax.broadcasted_iota(jnp.int32, (BLK, width), 0)
        kpos = lax.broadcasted_iota(jnp.int32, (BLK, width), 1)
        sel = _topk_select(score, kpos <= qpos, topk)
        kb = k_ref[0:width, :].astype(BF16)
        vb = v_ref[0:width, :].astype(BF16)
        g = g_ref[...]
        for h in range(DSA_HEADS):
            qh = q_ref[:, h * DSA_HD:(h + 1) * DSA_HD].astype(BF16)
            lg = lax.dot_general(qh, kb, NT, preferred_element_type=F32) * (DSA_HD ** -0.5)
            lg = jnp.where(sel, lg, -jnp.inf)
            e = jnp.exp(lg - jnp.max(lg, axis=1, keepdims=True))
            o = jnp.dot(e.astype(BF16), vb, preferred_element_type=F32) / jnp.sum(e, axis=1, keepdims=True)
            o_ref[:, h * DSA_HD:(h + 1) * DSA_HD] = (o * _silu(g[:, h * DSA_HD:(h + 1) * DSA_HD])).astype(o_ref.dtype)

    per = widths[0] // BLK
    for v, width in enumerate(widths):
        pl.when(i // per == v)(functools.partial(compute, width))


def _dsa_prompt(p, n, seq):
    nq = seq // BLK
    t = p.shape[0]
    topk = min(DSA_TOPK_MAX, seq // 4)
    nvar = math.gcd(nq, 8)
    widths = tuple((v + 1) * (nq // nvar) * BLK for v in range(nvar))
    body = functools.partial(_dsa_prompt_body, topk=topk, widths=widths)
    rowblk = lambda b, i: b * nq + i
    return pl.pallas_call(
        body, grid=(n, nq),
        in_specs=[pl.BlockSpec((BLK, GROUP_W), lambda b, i: (rowblk(b, i), _COL['dsa_q'] // 4)),
                  pl.BlockSpec((BLK, GROUP_W), lambda b, i: (rowblk(b, i), _COL['idx_q'] // 4)),
                  pl.BlockSpec((BLK, LANE), lambda b, i: (rowblk(b, i), _COL['idx_w'])),
                  pl.BlockSpec((BLK, GROUP_W), lambda b, i: (rowblk(b, i), _COL['dsa_g'] // 4)),
                  pl.BlockSpec((seq, DSA_HD), lambda b, i: (b, _COL['dsa_k'])),
                  pl.BlockSpec((seq, DSA_HD), lambda b, i: (b, _COL['dsa_v'])),
                  pl.BlockSpec((seq, LANE), lambda b, i: (b, _COL['idx_k']))],
        out_specs=pl.BlockSpec((BLK, GROUP_W), lambda b, i: (rowblk(b, i), 0)),
        out_shape=jax.ShapeDtypeStruct((t, GROUP_W), BF16), name="dsa_prompt")(p, p, p, p, p, p, p)


def _dsa_sample_body(pt_ref, q_ref, iq_ref, iw_ref, g_ref, kn_ref, vn_ref, ikn_ref, *rest,
                     npages, seq_pad, valid_len, topk):
    pages = rest[:npages]
    o_ref = rest[npages]
    nblk = npages + 1
    iq = iq_ref[...]
    iqs = jnp.concatenate([iq[:, h * IDX_HD:(h + 1) * IDX_HD] for h in range(IDX_HEADS)], axis=0).astype(BF16)
    iw = iw_ref[...] * (IDX_HEADS ** -0.5)
    zpad = lambda x: jnp.concatenate([x, jnp.zeros((BLK - seq_pad, x.shape[1]), F32)], axis=0).astype(BF16)

    def against_keys(x, b, lo, hi, new_ref):
        if b < npages:
            return jnp.dot(x, pages[b][0, lo:hi, :].astype(BF16), preferred_element_type=F32)
        return lax.dot_general(x, zpad(new_ref[:, 0:hi - lo]), NT, preferred_element_type=F32)

    def times_values(e, b):
        if b < npages:
            return lax.dot_general(e, pages[b][0, DSA_HD:2 * DSA_HD, :].astype(BF16), NT, preferred_element_type=F32)
        return jnp.dot(e, zpad(vn_ref[...]), preferred_element_type=F32)

    pieces = []
    for b in range(nblk):
        s = against_keys(iqs, b, 2 * DSA_HD, DSA_CACHE_W, ikn_ref)
        rel = jnp.maximum(s, 0.0) * (IDX_HD ** -0.5)
        sc = jnp.zeros((seq_pad, BLK), F32)
        for h in range(IDX_HEADS):
            sc = sc + rel[h * seq_pad:(h + 1) * seq_pad] * iw[:, h:h + 1]
        pieces.append(sc)
    score = jnp.concatenate(pieces, axis=1)
    tq = lax.broadcasted_iota(jnp.int32, (seq_pad, nblk * BLK), 0)
    kk = lax.broadcasted_iota(jnp.int32, (seq_pad, nblk * BLK), 1)
    tnew = kk - npages * BLK
    vis = (tnew < 0) | ((tnew <= tq) & (tnew < valid_len))
    sel = _topk_select(score, vis, topk)
    q = q_ref[...]
    q4 = jnp.concatenate([q[:, h * DSA_HD:(h + 1) * DSA_HD] for h in range(DSA_HEADS)], axis=0).astype(BF16)
    lgs = []
    for b in range(nblk):
        lg = against_keys(q4, b, 0, DSA_HD, kn_ref) * (DSA_HD ** -0.5)
        sb = sel[:, b * BLK:(b + 1) * BLK]
        lgs.append(jnp.where(jnp.concatenate([sb] * DSA_HEADS, axis=0), lg, -jnp.inf))
    mx = lgs[0].max(axis=1, keepdims=True)
    for lg in lgs[1:]:
        mx = jnp.maximum(mx, lg.max(axis=1, keepdims=True))
    den = jnp.zeros((DSA_HEADS * seq_pad, 1), F32)
    acc = jnp.zeros((DSA_HEADS * seq_pad, DSA_HD), F32)
    for b in range(nblk):
        e = jnp.exp(lgs[b] - mx)
        den = den + jnp.sum(e, axis=1, keepdims=True)
        acc = acc + times_values(e.astype(BF16), b)
    o = acc / den
    o = jnp.concatenate([o[h * seq_pad:(h + 1) * seq_pad] for h in range(DSA_HEADS)], axis=1)
    o_ref[...] = (o * _silu(g_ref[...])).astype(o_ref.dtype)


def _dsa_sample(p, cache, page_table, n, seq_pad, valid_len):
    npages = page_table.shape[1]
    t = p.shape[0]
    lk = npages * BLK + valid_len
    topk = min(DSA_TOPK_MAX, lk // 4)
    body = functools.partial(_dsa_sample_body, npages=npages, seq_pad=seq_pad, valid_len=valid_len, topk=topk)
    rows = lambda c: pl.BlockSpec((seq_pad, GROUP_W), lambda b, pt: (b, c // 4))
    rows1 = lambda c: pl.BlockSpec((seq_pad, LANE), lambda b, pt: (b, c))
    page_specs = [pl.BlockSpec((1, DSA_CACHE_W, BLK), functools.partial(lambda b, pt, a: (pt[b, a], 0, 0), a=a))
                  for a in range(npages)]
    grid_spec = pltpu.PrefetchScalarGridSpec(
        num_scalar_prefetch=1, grid=(n,),
        in_specs=[rows(_COL['dsa_q']), rows(_COL['idx_q']), rows1(_COL['idx_w']), rows(_COL['dsa_g']),
                  rows1(_COL['dsa_k']), rows1(_COL['dsa_v']), rows1(_COL['idx_k'])] + page_specs,
        out_specs=pl.BlockSpec((seq_pad, GROUP_W), lambda b, pt: (b, 0)))
    return pl.pallas_call(body, grid_spec=grid_spec, out_shape=jax.ShapeDtypeStruct((t, GROUP_W), BF16),
                          name="dsa_sample")(page_table, p, p, p, p, p, p, p, *([cache] * npages))


def _head_sum(x, ones_bd):
    hi = x.astype(BF16)
    lo = (x - hi.astype(F32)).astype(BF16)
    return jnp.dot(hi, ones_bd, preferred_element_type=F32) + jnp.dot(lo, ones_bd, preferred_element_type=F32)


def _softplus(x):
    return jnp.maximum(x, 0.0) + jnp.log1p(jnp.exp(-jnp.abs(x)))


def _rw_prep_body(ps_ref, prev_ref, first_ref, mu_ref, w0_ref, w2_ref, a0_ref, a2_ref, kkw_ref, kaw_ref, ones_ref,
                  r_ref, w_ref, k_ref, v_ref, kk_ref, ka_ref, *, n_seq, seq_len, n_inner):
    rows = n_seq * seq_len
    ps = ps_ref[...]
    xprev = pltpu.roll(ps, 1, 0)
    row = lax.broadcasted_iota(jnp.int32, (rows, SHIFT_P), 0)
    if n_seq == 1:
        c = pl.program_id(0) % n_inner
        first = jnp.where(c == 0, first_ref[0], prev_ref[SUBLANE - 1:SUBLANE, :])
        xprev = jnp.where(row == 0, first, xprev)
    else:
        for s in range(n_seq):
            xprev = jnp.where(row == s * seq_len, first_ref[0, s:s + 1, :], xprev)
    xs = ps + (xprev - ps) * mu_ref[...]
    r = xs[:, 0:GROUP_W]
    k = xs[:, GROUP_W:2 * GROUP_W]
    v = xs[:, 2 * GROUP_W:3 * GROUP_W]
    wl = xs[:, 3 * GROUP_W:3 * GROUP_W + LANE]
    al = xs[:, 3 * GROUP_W + LANE:3 * GROUP_W + 2 * LANE]
    wlin = w0_ref[...] + jnp.dot(jnp.tanh(wl).astype(BF16), w2_ref[...], preferred_element_type=F32)
    w_raw = -_softplus(-wlin) - 0.5
    decay = jnp.exp(-jnp.exp(w_raw))
    a = jax.nn.sigmoid(a0_ref[...] + jnp.dot(al.astype(BF16), a2_ref[...], preferred_element_type=F32))
    kk = k * kkw_ref[...]
    k2 = k * (1.0 + (a - 1.0) * kaw_ref[...])
    nrm = jnp.sqrt(_head_sum(kk * kk, ones_ref[...]))
    kkn = kk / jnp.maximum(nrm, 1e-12)
    r_ref[...] = r
    w_ref[...] = decay
    k_ref[...] = k2
    v_ref[...] = v
    kk_ref[...] = kkn
    ka_ref[...] = kkn * a


def _rw_prep(p, first_prev, prm, *, n_seq, seq_len, n_outer, n_inner):
    rows = n_seq * seq_len
    t = p.shape[0]
    nb = t // rows
    sub_per_blk = rows // SUBLANE
    body = functools.partial(_rw_prep_body, n_seq=n_seq, seq_len=seq_len, n_inner=n_inner)
    c1 = lambda i: (0, 0)
    vec = pl.BlockSpec((1, GROUP_W), c1)
    out = pl.BlockSpec((rows, GROUP_W), lambda i: (i, 0))
    return pl.pallas_call(
        body, grid=(nb,),
        in_specs=[pl.BlockSpec((rows, SHIFT_P), lambda i: (i, _COL['rw_shift'] // 16)),
                  pl.BlockSpec((SUBLANE, SHIFT_P), lambda i: (jnp.maximum(i * sub_per_blk - 1, 0), _COL['rw_shift'] // 16)),
                  pl.BlockSpec((1, n_seq, SHIFT_P), lambda i: (i // n_inner, 0, 0)),
                  pl.BlockSpec((1, SHIFT_P), c1), vec, pl.BlockSpec((LANE, GROUP_W), c1), vec,
                  pl.BlockSpec((LANE, GROUP_W), c1), vec, vec, pl.BlockSpec((GROUP_W, GROUP_W), c1)],
        out_specs=[out] * 6,
        out_shape=[jax.ShapeDtypeStruct((t, GROUP_W), F32)] * 6,
        name="rw_prep")(p, p, first_prev, prm['mu'], prm['w0'], prm['w2'], prm['a0'], prm['a2'],
                        prm['kk'], prm['ka'], prm['ones_bd'])


def _rw_scan_body(w_ref, kk_ref, ka_ref, k_ref, r_ref, v_ref, s0_ref, y_ref, sout_ref, s_ref, *, tb, nj):
    @pl.when(pl.program_id(1) == 0)
    def _():
        s_ref[...] = s0_ref[...]

    si = s_ref.shape[1]
    unroll = 8

    def token(t, carry):
        def p1(j, sa):
            return sa + s_ref[j] * kk_ref[t, pl.ds(j, 1), :]
        sa = -lax.fori_loop(0, nj, p1, jnp.zeros((si, LANE), F32), unroll=unroll)
        vt = v_ref[t]

        def p2(j, y):
            s_new = (s_ref[j] * w_ref[t, pl.ds(j, 1), :] + sa * ka_ref[t, pl.ds(j, 1), :]
                     + vt * k_ref[t, pl.ds(j, 1), :])
            s_ref[j] = s_new
            return y + s_new * r_ref[t, pl.ds(j, 1), :]
        y_ref[t] = lax.fori_loop(0, nj, p2, jnp.zeros((si, LANE), F32), unroll=unroll)
        return carry

    lax.fori_loop(0, tb, token, 0)
    sout_ref[...] = s_ref[...]


def _rw_scan(rowvecs, v, s0, *, tb):
    seq, nj, lanes = rowvecs[0].shape
    si = v.shape[1]
    ngroups = lanes // LANE
    body = functools.partial(_rw_scan_body, tb=tb, nj=nj)
    rv = pl.BlockSpec((tb, nj, LANE), lambda g, t: (t, 0, g))
    cv = pl.BlockSpec((tb, si, LANE), lambda g, t: (t, 0, g))
    st = pl.BlockSpec((nj, si, LANE), lambda g, t: (0, 0, g))
    return pl.pallas_call(
        body, grid=(ngroups, seq // tb),
        in_specs=[rv] * 5 + [cv, st],
        out_specs=[cv, st],
        out_shape=[jax.ShapeDtypeStruct((seq, si, lanes), F32), jax.ShapeDtypeStruct((nj, si, lanes), F32)],
        scratch_shapes=[pltpu.VMEM((nj, si, LANE), F32)],
        name="rw_scan")(*rowvecs, v, s0)


def _rw_post_body(y_ref, r_ref, k_ref, v_ref, g_ref, rk_ref, lnw_ref, lnb_ref, ones_ref, o_ref):
    ones_bd = ones_ref[...]
    y = y_ref[...]
    mu = _head_sum(y, ones_bd) * (1.0 / RW_HD)
    d = y - mu
    var = _head_sum(d * d, ones_bd) * (1.0 / RW_HD)
    yn = d * lax.rsqrt(var + RW_LN_EPS) * lnw_ref[...] + lnb_ref[...]
    v = v_ref[...]
    out = yn + _head_sum(r_ref[...] * k_ref[...] * rk_ref[...], ones_bd) * v
    o_ref[...] = (out * _silu(g_ref[...])).astype(o_ref.dtype)


def _rw_post(y, r, k, v, p, prm, rows=BLK):
    t = y.shape[0]
    blk = pl.BlockSpec((rows, GROUP_W), lambda i: (i, 0))
    vec = pl.BlockSpec((1, GROUP_W), lambda i: (0, 0))
    return pl.pallas_call(
        _rw_post_body, grid=(t // rows,),
        in_specs=[blk, blk, blk, blk, pl.BlockSpec((rows, GROUP_W), lambda i: (i, _COL['rw_g'] // 4)),
                  vec, vec, vec, pl.BlockSpec((GROUP_W, GROUP_W), lambda i: (0, 0))],
        out_specs=blk, out_shape=jax.ShapeDtypeStruct((t, GROUP_W), BF16),
        name="rw_post")(y, r, k, v, p, prm['rk'], prm['lnw'], prm['lnb'], prm['ones_bd'])


def _permute_shift(x):
    o1, o2, o3, o4 = GROUP_W, GROUP_W + RW_LORA, 2 * GROUP_W + RW_LORA, 3 * GROUP_W + RW_LORA
    z = lambda n: jnp.zeros(x.shape[:-1] + (n,), x.dtype)
    return jnp.concatenate([x[..., :o1], x[..., o2:o3], x[..., o3:o4], x[..., o1:o2], z(LANE - RW_LORA),
                            x[..., o4:], z(LANE - RW_LORA), z(2 * LANE)], axis=-1)


def _unpermute_shift(x):
    g = GROUP_W
    return jnp.concatenate([x[..., :g], x[..., 3 * g:3 * g + RW_LORA], x[..., g:2 * g], x[..., 2 * g:3 * g],
                            x[..., 3 * g + LANE:3 * g + LANE + RW_LORA]], axis=-1)


def _layout_w_in(w):
    d = w.shape[0]
    off = {}
    o = 0
    for name, n in (('ret', 3 * GROUP_W), ('sb', 4 * GROUP_W), ('dsa_q', GROUP_W), ('dsa_k', DSA_HD), ('dsa_v', DSA_HD),
                    ('idx_q', IDX_HEADS * IDX_HD), ('idx_k', IDX_HD), ('idx_w', IDX_HEADS), ('dsa_g', GROUP_W),
                    ('rw_shift', SHIFT_W), ('rw_g', GROUP_W)):
        off[name] = (o, o + n)
        o += n
    sl = lambda name: w[:, off[name][0]:off[name][1]]
    z = lambda n: jnp.zeros((d, n), w.dtype)
    parts = [sl('ret'), sl('sb'), sl('dsa_q'), sl('idx_q'), sl('dsa_g'), sl('dsa_k'), sl('dsa_v'),
             sl('idx_k'), sl('idx_k'), sl('idx_w'), z(LANE - IDX_HEADS), sl('rw_g'), _permute_shift(sl('rw_shift'))]
    out = jnp.concatenate(parts, axis=1).astype(BF16)
    assert out.shape[1] == N_COLS
    return out


def _pad_lora(w2):
    return jnp.concatenate([w2, jnp.zeros((LANE - RW_LORA, GROUP_W), w2.dtype)], axis=0).astype(BF16)


def _layer_params(l, norm_g, w_in, shift_mu, rw_w0, rw_w2, rw_a0, rw_a2, rw_kk, rw_ka, rw_rk, rw_lnx_w, rw_lnx_b, w_out):
    head = np.arange(GROUP_W) // RW_HD
    row = lambda x: x.reshape(1, -1)
    return dict(g=norm_g[l], w_in=_layout_w_in(w_in[l]), w_out=w_out[l].astype(BF16),
                mu=row(_permute_shift(shift_mu[l])), w0=row(rw_w0[l]), w2=_pad_lora(rw_w2[l]),
                a0=row(rw_a0[l]), a2=_pad_lora(rw_a2[l]), kk=row(rw_kk[l]), ka=row(rw_ka[l]),
                rk=row(rw_rk[l]), lnw=row(rw_lnx_w[l]), lnb=row(rw_lnx_b[l]),
                ones_bd=jnp.asarray(head[:, None] == head[None, :], BF16))


def _scan_prompt(prep, n, seq):
    r, w, k, v, kk, ka = prep
    half = RW_HD // 2
    assert 2 * n * RW_HEADS == LANE

    def rowvec(x):
        y = x.reshape(n, seq, RW_HEADS, RW_HD).transpose(1, 3, 0, 2).reshape(seq, RW_HD, n * RW_HEADS)
        return jnp.concatenate([y, y], axis=-1)

    def colvec(x):
        return x.reshape(n, seq, RW_HEADS, 2, half).transpose(1, 4, 3, 0, 2).reshape(seq, half, LANE)

    s0 = jnp.zeros((RW_HD, half, LANE), F32)
    y, s = _rw_scan([rowvec(w), rowvec(kk), rowvec(ka), rowvec(k), rowvec(r)], colvec(v), s0, tb=32)
    y = y.reshape(seq, half, 2, n, RW_HEADS).transpose(3, 0, 4, 2, 1).reshape(n * seq, GROUP_W)
    s = s.reshape(RW_HD, half, 2, n, RW_HEADS).transpose(3, 4, 2, 1, 0).reshape(n, RW_HEADS, RW_HD, RW_HD)
    return y, s


def _scan_sample(prep, s0, n, seq_pad, valid_len):
    r, w, k, v, kk, ka = prep

    def vec(x):
        return x.reshape(n, seq_pad, RW_HEADS, RW_HD)[:, :valid_len].transpose(1, 3, 0, 2).reshape(
            valid_len, RW_HD, n * RW_HEADS)

    s0 = s0.transpose(3, 2, 0, 1).reshape(RW_HD, RW_HD, n * RW_HEADS)
    y, s = _rw_scan([vec(w), vec(kk), vec(ka), vec(k), vec(r)], vec(v), s0, tb=valid_len)
    y = y.reshape(valid_len, RW_HD, n, RW_HEADS).transpose(2, 0, 3, 1)
    y = jnp.pad(y, ((0, 0), (0, seq_pad - valid_len), (0, 0), (0, 0))).reshape(n * seq_pad, GROUP_W)
    s = s.reshape(RW_HD, RW_HD, n, RW_HEADS).transpose(2, 3, 1, 0)
    return y, s


def _prompt_layer(x, h, prm, n, seq, tables, last_g, last_dtype):
    nq = seq // BLK
    p = _in_proj(h, prm['w_in'])
    ret_o, ret_s = _retention(p, jnp.zeros((n, RET_HEADS * RET_DK, RET_DV), F32), tables,
                              n_seq=1, seq_len=BLK, n_outer=n, n_inner=nq, table_by_inner=True)
    sb_o = _sb_prompt(p, n, seq)
    dsa_o = _dsa_prompt(p, n, seq)
    prep = _rw_prep(p, jnp.zeros((n, 1, SHIFT_P), F32), prm, n_seq=1, seq_len=BLK, n_outer=n, n_inner=nq)
    y, wkv_s = _scan_prompt(prep, n, seq)
    rw_o = _rw_post(y, prep[0], prep[2], prep[3], p, prm)
    mix = jnp.concatenate([ret_o, sb_o, dsa_o, rw_o], axis=1)
    x, h = _out_proj(x, mix, prm['w_out'], last_g, last_dtype)
    p3 = p.reshape(n, seq, N_COLS)
    c = lambda name: _COL[name] * LANE
    sb_new = p3[:, :, c('sb_k'):c('sb_g')].reshape(n, seq, 2, SB_HEADS, SB_HD)
    dsa_new = p3[:, :, c('dsa_k'):c('dsa_k') + DSA_CACHE_W]
    shift = _unpermute_shift(p3[:, seq - 1, c('rw_shift'):])
    return x, h, (sb_new, dsa_new, ret_s.reshape(n, RET_HEADS, RET_DK, RET_DV), wkv_s, shift)


def _sample_layer(x, h, prm, n, seq_pad, valid_len, tables, past, page_table, last_g, last_dtype):
    shift_prev, ret_s0, wkv_s0, sb_cache, dsa_cache = past
    n_seq = BLK // seq_pad
    nb = n // n_seq
    p = _in_proj(h, prm['w_in'])
    ret_o, ret_s = _retention(p, ret_s0.reshape(n, RET_HEADS * RET_DK, RET_DV), tables,
                              n_seq=n_seq, seq_len=seq_pad, n_outer=nb, n_inner=1, table_by_inner=False)
    sb_o = _sb_sample(p, sb_cache, page_table, n, seq_pad, valid_len)
    dsa_o = _dsa_sample(p, dsa_cache, page_table, n, seq_pad, valid_len)
    first_prev = _permute_shift(shift_prev).reshape(nb, n_seq, SHIFT_P)
    prep = _rw_prep(p, first_prev, prm, n_seq=n_seq, seq_len=seq_pad, n_outer=nb, n_inner=1)
    y, wkv_s = _scan_sample(prep, wkv_s0, n, seq_pad, valid_len)
    rw_o = _rw_post(y, prep[0], prep[2], prep[3], p, prm)
    mix = jnp.concatenate([ret_o, sb_o, dsa_o, rw_o], axis=1)
    x, h = _out_proj(x, mix, prm['w_out'], last_g, last_dtype)
    p3 = p.reshape(n, seq_pad, N_COLS)[:, :valid_len]
    c = lambda name: _COL[name] * LANE
    sb_new = p3[:, :, c('sb_k'):c('sb_g')].reshape(n, valid_len, 2, SB_HEADS, SB_HD)
    dsa_new = p3[:, :, c('dsa_k'):c('dsa_k') + DSA_CACHE_W]
    shift = _unpermute_shift(p3[:, valid_len - 1, c('rw_shift'):])
    return x, h, (sb_new, dsa_new, ret_s.reshape(n, RET_HEADS, RET_DK, RET_DV), wkv_s, shift)


def kernel(x_prompt, x_sample, cache_sb_kv, cache_dsa, state_ret, state_wkv, state_shift, page_table,
           norm_g, w_in, shift_mu, rw_w0, rw_w2, rw_a0, rw_a2, rw_kk, rw_ka, rw_rk, rw_lnx_w, rw_lnx_b,
           w_out, final_g):
    n_p, seq, d = x_prompt.shape
    n_s, dec_seq, _ = x_sample.shape
    depth = norm_g.shape[0]
    past_len = page_table.shape[1] * cache_sb_kv.shape[2]
    assert cache_sb_kv.shape[2] == BLK and seq % BLK == 0 and d == 4 * GROUP_W
    seq_pad = SUBLANE
    assert dec_seq <= seq_pad and (n_s * seq_pad) % BLK == 0
    n_seq = BLK // seq_pad

    prms = [_layer_params(l, norm_g, w_in, shift_mu, rw_w0, rw_w2, rw_a0, rw_a2, rw_kk, rw_ka, rw_rk,
                          rw_lnx_w, rw_lnx_b, w_out) for l in range(depth)]
    pos = np.arange(seq)
    rows = np.arange(BLK)
    tables_p = _rotary_tables(pos) + _decay_tables(rows, np.zeros(BLK, np.int64), float(BLK))
    tables_s = (_rotary_tables(past_len + rows % seq_pad)
                + _decay_tables(rows % seq_pad, rows // seq_pad, float(dec_seq)))

    n_pool = cache_sb_kv.shape[1]
    sb_pages = cache_sb_kv.reshape(depth * n_pool * SB_PAGE_ROWS, SB_HD)
    dsa_pages = cache_dsa.transpose(0, 1, 3, 2).reshape(depth * n_pool, DSA_CACHE_W, BLK)
    xp = x_prompt.reshape(n_p * seq, d)
    xs = jnp.pad(x_sample, ((0, 0), (0, seq_pad - dec_seq), (0, 0))).reshape(n_s * seq_pad, d)
    hp = _rmsnorm(xp, prms[0]['g'], BF16)
    hs = _rmsnorm(xs, prms[0]['g'], BF16)
    outs_p, outs_s = [], []
    for l in range(depth):
        last = l == depth - 1
        g_next = final_g if last else prms[l + 1]['g']
        dt = F32 if last else BF16
        xp, hp, st = _prompt_layer(xp, hp, prms[l], n_p, seq, tables_p, g_next, dt)
        outs_p.append(st)
        past = (state_shift[l], state_ret[l], state_wkv[l], sb_pages, dsa_pages)
        xs, hs, st = _sample_layer(xs, hs, prms[l], n_s, seq_pad, dec_seq, tables_s, past,
                                   page_table + l * n_pool, g_next, dt)
        outs_s.append(st)
    y_prompt = hp.reshape(n_p, seq, d)
    y_sample = hs.reshape(n_s, seq_pad, d)[:, :dec_seq]
    stack = lambda outs, i: jnp.stack([o[i] for o in outs])
    return (y_prompt, y_sample, *[stack(outs_p, i) for i in range(5)], *[stack(outs_s, i) for i in range(5)])
```

```python
import functools
import math

import jax
import jax.numpy as jnp
import numpy as np
from jax import lax
from jax.experimental import pallas as pl
from jax.experimental.pallas import tpu as pltpu

F32 = jnp.float32
BF16 = jnp.bfloat16
LANE = 128
SUBLANE = 8

GROUP_W = 512
RET_HEADS, RET_DK, RET_DV = 4, 64, 128
SB_HEADS, SB_HD = 4, 128
DSA_HEADS, DSA_HD = 4, 128
IDX_HEADS, IDX_HD = 8, 64
DSA_TOPK_MAX = 256
DSA_CACHE_W = 2 * DSA_HD + IDX_HD
RW_HD, RW_HEADS = 64, 8
RW_LORA = 96
SHIFT_W = 3 * GROUP_W + 2 * RW_LORA
RW_LN_EPS = 64e-5
NORM_EPS = 1e-6
ROPE_BASE = 10000.0
BLK = 128

_COL = dict(ret_q=0, ret_k=2, ret_v=4, ret_g=8, sb_q=12, sb_k=16, sb_v=20, sb_g=24,
            dsa_q=28, idx_q=32, dsa_g=36, dsa_k=40, dsa_v=41, idx_k=42, idx_w=43,
            rw_g=44, rw_shift=48)
N_COLS = 64 * LANE
SHIFT_P = 16 * LANE
NT = (((1,), (1,)), ((), ()))
TN = (((0,), (0,)), ((), ()))
INT_MIN = -2 ** 31


def _silu(x):
    return x * jax.nn.sigmoid(x)


def _rmsnorm_body(x_ref, g_ref, o_ref):
    x = x_ref[...]
    y = x * lax.rsqrt(jnp.mean(x * x, axis=-1, keepdims=True) + NORM_EPS)
    o_ref[...] = (y * g_ref[...]).astype(o_ref.dtype)


def _rmsnorm(x, g, out_dtype, tm=256):
    t, d = x.shape
    tm = min(tm, t)
    return pl.pallas_call(
        _rmsnorm_body, grid=(t // tm,),
        in_specs=[pl.BlockSpec((tm, d), lambda i: (i, 0)), pl.BlockSpec((1, d), lambda i: (0, 0))],
        out_specs=pl.BlockSpec((tm, d), lambda i: (i, 0)),
        out_shape=jax.ShapeDtypeStruct((t, d), out_dtype), name="rmsnorm")(x, g.reshape(1, d))


def _matmul_body(a_ref, b_ref, o_ref):
    o_ref[...] = jnp.dot(a_ref[...], b_ref[...], preferred_element_type=F32)


def _in_proj(h, w, tm=512, tn=1024):
    t, d = h.shape
    nc = w.shape[1]
    tm = min(tm, t)
    return pl.pallas_call(
        _matmul_body, grid=(t // tm, nc // tn),
        in_specs=[pl.BlockSpec((tm, d), lambda i, j: (i, 0)), pl.BlockSpec((d, tn), lambda i, j: (0, j))],
        out_specs=pl.BlockSpec((tm, tn), lambda i, j: (i, j)),
        out_shape=jax.ShapeDtypeStruct((t, nc), F32), name="in_proj")(h, w)


def _out_proj_body(x_ref, m_ref, w_ref, g_ref, xo_ref, yo_ref):
    xn = x_ref[...] + jnp.dot(m_ref[...], w_ref[...], preferred_element_type=F32)
    xo_ref[...] = xn
    y = xn * lax.rsqrt(jnp.mean(xn * xn, axis=-1, keepdims=True) + NORM_EPS)
    yo_ref[...] = (y * g_ref[...]).astype(yo_ref.dtype)


def _out_proj(x, mix, w, g, y_dtype, tm=256):
    t, d = x.shape
    tm = min(tm, t)
    return pl.pallas_call(
        _out_proj_body, grid=(t // tm,),
        in_specs=[pl.BlockSpec((tm, d), lambda i: (i, 0)), pl.BlockSpec((tm, d), lambda i: (i, 0)),
                  pl.BlockSpec((d, d), lambda i: (0, 0)), pl.BlockSpec((1, d), lambda i: (0, 0))],
        out_specs=[pl.BlockSpec((tm, d), lambda i: (i, 0)), pl.BlockSpec((tm, d), lambda i: (i, 0))],
        out_shape=[jax.ShapeDtypeStruct((t, d), F32), jax.ShapeDtypeStruct((t, d), y_dtype)],
        name="out_proj")(x, mix, w, g.reshape(1, d))


def _ret_body(q_ref, k_ref, v_ref, g_ref, cos_ref, sin_ref, dmask_ref, qdec_ref, kdec_ref, sdec_ref,
              s0_ref, o_ref, sout_ref, ss_ref, *, n_seq, seq_len):
    rows = n_seq * seq_len

    @pl.when(pl.program_id(1) == 0)
    def _():
        ss_ref[...] = s0_ref[...]

    lane = lax.broadcasted_iota(jnp.int32, (rows, RET_HEADS * RET_DK), 1)
    first_half = (lane % RET_DK) < (RET_DK // 2)
    cosf, sinf = cos_ref[...], sin_ref[...]
    width = RET_HEADS * RET_DK

    def rot(x):
        swapped = jnp.where(first_half, pltpu.roll(x, width - RET_DK // 2, 1), pltpu.roll(x, RET_DK // 2, 1))
        return x * cosf + swapped * sinf

    qr = rot(q_ref[...])
    kr = rot(k_ref[...]) * (RET_DK ** -0.5)
    kr_b = kr.astype(BF16)
    kd = kr * kdec_ref[...]
    v = v_ref[...]
    g = g_ref[...]
    qdec = qdec_ref[...]
    upd = [jnp.zeros((width, RET_DV), F32) for _ in range(n_seq)]
    for h in range(RET_HEADS):
        mh = (lane // RET_DK) == h
        qh = jnp.where(mh, qr, 0.0).astype(BF16)
        att = lax.dot_general(qh, kr_b, NT, preferred_element_type=F32) * dmask_ref[h]
        vh = v[:, h * RET_DV:(h + 1) * RET_DV].astype(BF16)
        o = jnp.dot(att.astype(BF16), vh, preferred_element_type=F32)
        inter = [jnp.dot(qh[s * seq_len:(s + 1) * seq_len], ss_ref[s].astype(BF16), preferred_element_type=F32)
                 for s in range(n_seq)]
        inter = inter[0] if n_seq == 1 else jnp.concatenate(inter, axis=0)
        o = o + inter * qdec[:, h * RET_DV:(h + 1) * RET_DV]
        o = o * lax.rsqrt(jnp.mean(o * o, axis=-1, keepdims=True) + NORM_EPS)
        o_ref[:, h * RET_DV:(h + 1) * RET_DV] = (o * _silu(g[:, h * RET_DV:(h + 1) * RET_DV])).astype(o_ref.dtype)
        kdh = jnp.where(mh, kd, 0.0).astype(BF16)
        for s in range(n_seq):
            upd[s] = upd[s] + lax.dot_general(kdh[s * seq_len:(s + 1) * seq_len], vh[s * seq_len:(s + 1) * seq_len],
                                              TN, preferred_element_type=F32)
    sdec = sdec_ref[...]
    for s in range(n_seq):
        new = ss_ref[s] * sdec + upd[s]
        ss_ref[s] = new
        sout_ref[s] = new


def _rotary_tables(pos_abs):
    half = RET_DK // 2
    inv = 1.0 / (ROPE_BASE ** (jnp.arange(half, dtype=F32) / half))
    ang = jnp.asarray(pos_abs, F32)[:, None] * inv[None, :]
    cos, sin = jnp.cos(ang), jnp.sin(ang)
    cosf = jnp.tile(jnp.concatenate([cos, cos], axis=1), (1, RET_HEADS))
    sinf = jnp.tile(jnp.concatenate([-sin, sin], axis=1), (1, RET_HEADS))
    return cosf, sinf


def _decay_tables(pos_in_seq, seq_id, valid_len):
    log_g = jnp.log1p(-jnp.exp2(-5.0 - jnp.arange(RET_HEADS, dtype=F32)))
    i = jnp.asarray(pos_in_seq, F32)
    sid = jnp.asarray(seq_id)
    diff = i[:, None] - i[None, :]
    same = sid[:, None] == sid[None, :]
    dmask = jnp.where((diff >= 0)[None] & same[None],
                      jnp.exp(jnp.maximum(diff, 0.0)[None] * log_g[:, None, None]), 0.0)
    q_dec = jnp.exp((i + 1.0)[:, None] * log_g[None, :])
    k_dec = jnp.where((i < valid_len)[:, None], jnp.exp((valid_len - 1.0 - i)[:, None] * log_g[None, :]), 0.0)
    s_dec = jnp.exp(valid_len * log_g)
    q_dec = jnp.repeat(q_dec, RET_DV, axis=1)
    k_dec = jnp.repeat(k_dec, RET_DK, axis=1)
    s_dec = jnp.broadcast_to(jnp.repeat(s_dec, RET_DK)[:, None], (RET_HEADS * RET_DK, RET_DV))
    return dmask, q_dec, k_dec, s_dec


def _retention(p, s0, tables, *, n_seq, seq_len, n_outer, n_inner, table_by_inner):
    rows = n_seq * seq_len
    cosf, sinf, dmask, q_dec, k_dec, s_dec = tables
    t = p.shape[0]
    width = RET_HEADS * RET_DK
    rb = lambda o, c: o * n_inner + c
    tb = (lambda o, c: (c, 0)) if table_by_inner else (lambda o, c: (0, 0))
    const2 = lambda o, c: (0, 0)
    body = functools.partial(_ret_body, n_seq=n_seq, seq_len=seq_len)
    return pl.pallas_call(
        body, grid=(n_outer, n_inner),
        in_specs=[pl.BlockSpec((rows, width), lambda o, c: (rb(o, c), _COL['ret_q'] // 2)),
                  pl.BlockSpec((rows, width), lambda o, c: (rb(o, c), _COL['ret_k'] // 2)),
                  pl.BlockSpec((rows, GROUP_W), lambda o, c: (rb(o, c), _COL['ret_v'] // 4)),
                  pl.BlockSpec((rows, GROUP_W), lambda o, c: (rb(o, c), _COL['ret_g'] // 4)),
                  pl.BlockSpec((rows, width), tb), pl.BlockSpec((rows, width), tb),
                  pl.BlockSpec((RET_HEADS, rows, rows), lambda o, c: (0, 0, 0)),
                  pl.BlockSpec((rows, GROUP_W), const2), pl.BlockSpec((rows, width), const2),
                  pl.BlockSpec((width, RET_DV), const2),
                  pl.BlockSpec((n_seq, width, RET_DV), lambda o, c: (o, 0, 0))],
        out_specs=[pl.BlockSpec((rows, GROUP_W), lambda o, c: (rb(o, c), 0)),
                   pl.BlockSpec((n_seq, width, RET_DV), lambda o, c: (o, 0, 0))],
        out_shape=[jax.ShapeDtypeStruct((t, GROUP_W), BF16),
                   jax.ShapeDtypeStruct(s0.shape, F32)],
        scratch_shapes=[pltpu.VMEM((n_seq, width, RET_DV), F32)],
        name="retention")(p, p, p, p, cosf, sinf, dmask, q_dec, k_dec, s_dec, s0)


def _sb_scores(z, vis, run, tri):
    lp = jnp.log1p(jnp.exp(-jnp.abs(z)))
    ls = jnp.minimum(z, 0.0) - lp
    lm = jnp.minimum(-z, 0.0) - lp
    l = jnp.where(vis, lm, 0.0)
    l_hi = l.astype(BF16)
    l_lo = (l - l_hi.astype(F32)).astype(BF16)
    after = run + jnp.dot(l_hi, tri, preferred_element_type=F32) + jnp.dot(l_lo, tri, preferred_element_type=F32)
    w = jnp.where(vis, jnp.exp(ls + after), 0.0)
    return w, run + jnp.sum(l, axis=1, keepdims=True)


SB_DEAD = -104.0


def _sb_live(run):
    return (jnp.max(run) >= SB_DEAD).astype(jnp.int32)


def _strict_lower(n):
    r = lax.broadcasted_iota(jnp.int32, (n, n), 0)
    c = lax.broadcasted_iota(jnp.int32, (n, n), 1)
    return (r > c).astype(BF16)


def _sb_prompt_body(q_ref, k_ref, v_ref, g_ref, o_ref):
    i = pl.program_id(1)
    heads = range(SB_HEADS)
    head = lambda h: slice(h * SB_HD, (h + 1) * SB_HD)
    q = [q_ref[:, head(h)].astype(BF16) for h in heads]
    scale = SB_HD ** -0.5
    row = lax.broadcasted_iota(jnp.int32, (BLK, BLK), 0)
    col = lax.broadcasted_iota(jnp.int32, (BLK, BLK), 1)
    tri = (row > col).astype(BF16)

    def cond(carry):
        jj, _, _, live = carry
        return (jj <= i) & (live > 0)

    def step(carry):
        jj, acc, run, _ = carry
        j = i - jj
        start = pl.multiple_of(j * BLK, BLK)
        vis = (j < i) | (col < row)
        acc, run = list(acc), list(run)
        for h in heads:
            kj = k_ref[pl.ds(start, BLK), head(h)].astype(BF16)
            vj = v_ref[pl.ds(start, BLK), head(h)].astype(BF16)
            z = lax.dot_general(q[h], kj, NT, preferred_element_type=F32) * scale
            w, run[h] = _sb_scores(z, vis, run[h], tri)
            acc[h] = acc[h] + jnp.dot(w.astype(BF16), vj, preferred_element_type=F32)
        live = _sb_live(run[0])
        for h in heads[1:]:
            live = jnp.maximum(live, _sb_live(run[h]))
        return jj + 1, tuple(acc), tuple(run), live

    init = (jnp.int32(0), tuple(jnp.zeros((BLK, SB_HD), F32) for _ in heads),
            tuple(jnp.zeros((BLK, 1), F32) for _ in heads), jnp.int32(1))
    _, acc, _, _ = lax.while_loop(cond, step, init)
    for h in heads:
        o_ref[:, head(h)] = (acc[h] * _silu(g_ref[:, head(h)])).astype(o_ref.dtype)


def _sb_prompt(p, n, seq):
    nq = seq // BLK
    t = p.shape[0]
    rows = lambda c: pl.BlockSpec((BLK, GROUP_W), lambda b, i: (b * nq + i, c // 4))
    keys = lambda c: pl.BlockSpec((seq, GROUP_W), lambda b, i: (b, c // 4))
    return pl.pallas_call(
        _sb_prompt_body, grid=(n, nq),
        in_specs=[rows(_COL['sb_q']), keys(_COL['sb_k']), keys(_COL['sb_v']), rows(_COL['sb_g'])],
        out_specs=pl.BlockSpec((BLK, GROUP_W), lambda b, i: (b * nq + i, 0)),
        out_shape=jax.ShapeDtypeStruct((t, GROUP_W), BF16), name="sb_prompt")(p, p, p, p)


SB_PAGE_ROWS = BLK * 2 * SB_HEADS
SB_AHEAD = 2


def _sb_sample_body(pt_ref, q_ref, kn_ref, vn_ref, g_ref, cache_ref, o_ref, buf_ref, sem_ref, *,
                    seq_pad, valid_len, npages):
    b = pl.program_id(0)
    m = SB_HEADS * seq_pad
    scale = SB_HD ** -0.5
    tri = _strict_lower(BLK)
    q = q_ref[...]
    heads = range(SB_HEADS)
    head_cols = lambda x, h: x[:, h * SB_HD:(h + 1) * SB_HD]
    qh = [head_cols(q, h).astype(BF16) for h in heads]

    demand_slot = 2 * SB_AHEAD

    def page_copy(seq, j, slot):
        page = pt_ref[seq, npages - 1 - j]
        src = cache_ref.at[pl.ds(pl.multiple_of(page * SB_PAGE_ROWS, SB_PAGE_ROWS), SB_PAGE_ROWS), :]
        return pltpu.make_async_copy(src, buf_ref.at[slot], sem_ref.at[slot])

    ahead_slot = lambda seq, j: (seq % 2) * SB_AHEAD + j

    @pl.when(b == 0)
    def _():
        for a in range(SB_AHEAD):
            page_copy(b, a, ahead_slot(b, a)).start()

    @pl.when(b + 1 < pl.num_programs(0))
    def _():
        for a in range(SB_AHEAD):
            page_copy(b + 1, a, ahead_slot(b + 1, a)).start()

    def attend(kh, vh, vis, acc, run):
        z = jnp.concatenate([lax.dot_general(qh[h], kh[h].astype(BF16), NT, preferred_element_type=F32)
                             for h in heads], axis=0) * scale
        w, run = _sb_scores(z, vis, run, tri)
        w = w.astype(BF16)
        upd = jnp.concatenate([jnp.dot(w[h * seq_pad:(h + 1) * seq_pad], vh[h].astype(BF16),
                                       preferred_element_type=F32) for h in heads], axis=0)
        return acc + upd, run

    pad = jnp.zeros((BLK - seq_pad, GROUP_W), F32)
    kn = jnp.concatenate([kn_ref[...], pad], axis=0)
    vn = jnp.concatenate([vn_ref[...], pad], axis=0)
    tq = lax.broadcasted_iota(jnp.int32, (m, BLK), 0) % seq_pad
    tk = lax.broadcasted_iota(jnp.int32, (m, BLK), 1)
    acc, run = attend([head_cols(kn, h) for h in heads], [head_cols(vn, h) for h in heads],
                      (tk < tq) & (tk < valid_len), jnp.zeros((m, SB_HD), F32), jnp.zeros((m, 1), F32))
    all_visible = jnp.full((m, BLK), True)

    def cond(carry):
        j, _, _, live = carry
        return (j < npages) & (live > 0)

    def step(carry):
        j, acc, run, _ = carry

        @pl.when(j < SB_AHEAD)
        def _():
            page_copy(b, j, ahead_slot(b, j)).wait()

        @pl.when(j >= SB_AHEAD)
        def _():
            copy = page_copy(b, j, demand_slot)
            copy.start()
            copy.wait()

        slot = jnp.where(j < SB_AHEAD, ahead_slot(b, j), demand_slot)
        kh = [buf_ref[slot, pl.ds(h, BLK, stride=2 * SB_HEADS), :] for h in heads]
        vh = [buf_ref[slot, pl.ds(SB_HEADS + h, BLK, stride=2 * SB_HEADS), :] for h in heads]
        acc, run = attend(kh, vh, all_visible, acc, run)
        return j + 1, acc, run, _sb_live(run)

    j, acc, _, _ = lax.while_loop(cond, step, (jnp.int32(0), acc, run, _sb_live(run)))

    for a in range(SB_AHEAD):
        @pl.when(j <= a)
        def _(a=a):
            page_copy(b, a, ahead_slot(b, a)).wait()

    out = jnp.concatenate([acc[h * seq_pad:(h + 1) * seq_pad] for h in heads], axis=1)
    o_ref[...] = (out * _silu(g_ref[...])).astype(o_ref.dtype)


def _sb_sample(p, cache, page_table, n, seq_pad, valid_len):
    npages = page_table.shape[1]
    assert npages >= SB_AHEAD
    t = p.shape[0]
    body = functools.partial(_sb_sample_body, seq_pad=seq_pad, valid_len=valid_len, npages=npages)
    rows = lambda c: pl.BlockSpec((seq_pad, GROUP_W), lambda b, pt: (b, c // 4))
    grid_spec = pltpu.PrefetchScalarGridSpec(
        num_scalar_prefetch=1, grid=(n,),
        in_specs=[rows(_COL['sb_q']), rows(_COL['sb_k']), rows(_COL['sb_v']), rows(_COL['sb_g']),
                  pl.BlockSpec(memory_space=pl.ANY)],
        out_specs=pl.BlockSpec((seq_pad, GROUP_W), lambda b, pt: (b, 0)),
        scratch_shapes=[pltpu.VMEM((2 * SB_AHEAD + 1, SB_PAGE_ROWS, SB_HD), F32),
                        pltpu.SemaphoreType.DMA((2 * SB_AHEAD + 1,))])
    return pl.pallas_call(body, grid_spec=grid_spec, out_shape=jax.ShapeDtypeStruct((t, GROUP_W), BF16),
                          name="sb_sample")(page_table, p, p, p, p, cache)


def _topk_select(score, vis, topk, radix_bits):
    rows, width = score.shape
    score = jnp.where(vis, score, -jnp.inf)
    score = jnp.where(score == 0.0, 0.0, score)
    u = pltpu.bitcast(score, jnp.int32)
    key = u ^ ((u >> 31) & 0x7fffffff)
    int_min = jnp.int32(INT_MIN)
    n_steps = 32 // radix_bits
    few = jnp.sum(vis.astype(F32), axis=1, keepdims=True) <= topk

    def cond(carry):
        step, _, _, pending = carry
        return (step < n_steps) & (pending > 0)

    def search(carry):
        step, thr, open_rows, _ = carry
        shift = 32 - radix_bits * (step + 1)
        n_ok = jnp.zeros((rows, 1), jnp.int32)
        for c in range(1, 2 ** radix_bits):
            cand = thr | jnp.left_shift(jnp.int32(c), shift)
            cnt = jnp.sum((key >= (cand ^ int_min)).astype(F32), axis=1, keepdims=True)
            n_ok = n_ok + (cnt >= topk).astype(jnp.int32)
            open_rows = jnp.where(cnt == topk, 0, open_rows)
        thr = thr | jnp.left_shift(n_ok, shift)
        return step + 1, thr, open_rows, jnp.max(open_rows)

    open_rows = jnp.where(few, 0, 1)
    init = (jnp.int32(0), jnp.zeros((rows, 1), jnp.int32), open_rows, jnp.max(open_rows))
    _, thr, _, _ = lax.while_loop(cond, search, init)
    thr = thr ^ int_min
    gt = key > thr
    eq = key == thr
    need = topk - jnp.sum(gt.astype(F32), axis=1, keepdims=True)
    r = lax.broadcasted_iota(jnp.int32, (BLK, BLK), 0)
    c = lax.broadcasted_iota(jnp.int32, (BLK, BLK), 1)
    before = (r < c).astype(BF16)
    seen = jnp.zeros((rows, 1), F32)
    pieces = []
    for b in range(width // BLK):
        eq_b = eq[:, b * BLK:(b + 1) * BLK]
        eq_f = eq_b.astype(BF16)
        rank = seen + jnp.dot(eq_f, before, preferred_element_type=F32)
        pieces.append(gt[:, b * BLK:(b + 1) * BLK] | (eq_b & (rank < need)))
        seen = seen + jnp.sum(eq_f.astype(F32), axis=1, keepdims=True)
    return jnp.concatenate(pieces, axis=1) & vis


def _dsa_prompt_body(q_ref, iq_ref, iw_ref, g_ref, k_ref, v_ref, ik_ref, o_ref, *, topk, widths):
    i = pl.program_id(1)

    def compute(width):
        ikk = ik_ref[0:width, :].astype(BF16)
        lane = lax.broadcasted_iota(jnp.int32, (BLK, LANE), 1)
        iw = iw_ref[...] * (IDX_HEADS ** -0.5)
        score = jnp.zeros((BLK, width), F32)
        for h in range(IDX_HEADS):
            pair = iq_ref[:, (h // 2) * LANE:(h // 2 + 1) * LANE]
            iqh = jnp.where((lane // IDX_HD) == (h % 2), pair, 0.0).astype(BF16)
            rel = jnp.maximum(lax.dot_general(iqh, ikk, NT, preferred_element_type=F32), 0.0) * (IDX_HD ** -0.5)
            score = score + rel * iw[:, h:h + 1]
        qpos = i * BLK + lax.broadcasted_iota(jnp.int32, (BLK, width), 0)
        kpos = lax.broadcasted_iota(jnp.int32, (BLK, width), 1)
        sel = _topk_select(score, kpos <= qpos, topk, radix_bits=1)
        kb = k_ref[0:width, :].astype(BF16)
        vb = v_ref[0:width, :].astype(BF16)
        g = g_ref[...]
        for h in range(DSA_HEADS):
            qh = q_ref[:, h * DSA_HD:(h + 1) * DSA_HD].astype(BF16)
            lg = lax.dot_general(qh, kb, NT, preferred_element_type=F32) * (DSA_HD ** -0.5)
            lg = jnp.where(sel, lg, -jnp.inf)
            e = jnp.exp(lg - jnp.max(lg, axis=1, keepdims=True))
            o = jnp.dot(e.astype(BF16), vb, preferred_element_type=F32) / jnp.sum(e, axis=1, keepdims=True)
            o_ref[:, h * DSA_HD:(h + 1) * DSA_HD] = (o * _silu(g[:, h * DSA_HD:(h + 1) * DSA_HD])).astype(o_ref.dtype)

    per = widths[0] // BLK
    for v, width in enumerate(widths):
        pl.when(i // per == v)(functools.partial(compute, width))


def _dsa_prompt(p, n, seq):
    nq = seq // BLK
    t = p.shape[0]
    topk = min(DSA_TOPK_MAX, seq // 4)
    nvar = math.gcd(nq, 8)
    widths = tuple((v + 1) * (nq // nvar) * BLK for v in range(nvar))
    body = functools.partial(_dsa_prompt_body, topk=topk, widths=widths)
    rowblk = lambda b, i: b * nq + i
    return pl.pallas_call(
        body, grid=(n, nq),
        in_specs=[pl.BlockSpec((BLK, GROUP_W), lambda b, i: (rowblk(b, i), _COL['dsa_q'] // 4)),
                  pl.BlockSpec((BLK, GROUP_W), lambda b, i: (rowblk(b, i), _COL['idx_q'] // 4)),
                  pl.BlockSpec((BLK, LANE), lambda b, i: (rowblk(b, i), _COL['idx_w'])),
                  pl.BlockSpec((BLK, GROUP_W), lambda b, i: (rowblk(b, i), _COL['dsa_g'] // 4)),
                  pl.BlockSpec((seq, DSA_HD), lambda b, i: (b, _COL['dsa_k'])),
                  pl.BlockSpec((seq, DSA_HD), lambda b, i: (b, _COL['dsa_v'])),
                  pl.BlockSpec((seq, LANE), lambda b, i: (b, _COL['idx_k']))],
        out_specs=pl.BlockSpec((BLK, GROUP_W), lambda b, i: (rowblk(b, i), 0)),
        out_shape=jax.ShapeDtypeStruct((t, GROUP_W), BF16), name="dsa_prompt")(p, p, p, p, p, p, p)


def _dsa_sample_body(pt_ref, q_ref, iq_ref, iw_ref, g_ref, kn_ref, vn_ref, ikn_ref, *rest,
                     npages, seq_pad, valid_len, topk):
    pages = rest[:npages]
    o_ref = rest[npages]
    nblk = npages + 1
    iq = iq_ref[...]
    iqs = jnp.concatenate([iq[:, h * IDX_HD:(h + 1) * IDX_HD] for h in range(IDX_HEADS)], axis=0).astype(BF16)
    iw = iw_ref[...] * (IDX_HEADS ** -0.5)
    zpad = lambda x: jnp.concatenate([x, jnp.zeros((BLK - seq_pad, x.shape[1]), F32)], axis=0).astype(BF16)

    def against_keys(x, b, lo, hi, new_ref):
        if b < npages:
            return jnp.dot(x, pages[b][0, lo:hi, :].astype(BF16), preferred_element_type=F32)
        return lax.dot_general(x, zpad(new_ref[:, 0:hi - lo]), NT, preferred_element_type=F32)

    def times_values(e, b):
        if b < npages:
            return lax.dot_general(e, pages[b][0, DSA_HD:2 * DSA_HD, :].astype(BF16), NT, preferred_element_type=F32)
        return jnp.dot(e, zpad(vn_ref[...]), preferred_element_type=F32)

    pieces = []
    for b in range(nblk):
        s = against_keys(iqs, b, 2 * DSA_HD, DSA_CACHE_W, ikn_ref)
        rel = jnp.maximum(s, 0.0) * (IDX_HD ** -0.5)
        sc = jnp.zeros((seq_pad, BLK), F32)
        for h in range(IDX_HEADS):
            sc = sc + rel[h * seq_pad:(h + 1) * seq_pad] * iw[:, h:h + 1]
        pieces.append(sc)
    score = jnp.concatenate(pieces, axis=1)
    tq = lax.broadcasted_iota(jnp.int32, (seq_pad, nblk * BLK), 0)
    kk = lax.broadcasted_iota(jnp.int32, (seq_pad, nblk * BLK), 1)
    tnew = kk - npages * BLK
    vis = (tnew < 0) | ((tnew <= tq) & (tnew < valid_len))
    sel = _topk_select(score, vis, topk, radix_bits=4)
    q = q_ref[...]
    q4 = jnp.concatenate([q[:, h * DSA_HD:(h + 1) * DSA_HD] for h in range(DSA_HEADS)], axis=0).astype(BF16)
    lgs = []
    for b in range(nblk):
        lg = against_keys(q4, b, 0, DSA_HD, kn_ref) * (DSA_HD ** -0.5)
        sb = sel[:, b * BLK:(b + 1) * BLK]
        lgs.append(jnp.where(jnp.concatenate([sb] * DSA_HEADS, axis=0), lg, -jnp.inf))
    mx = lgs[0].max(axis=1, keepdims=True)
    for lg in lgs[1:]:
        mx = jnp.maximum(mx, lg.max(axis=1, keepdims=True))
    den = jnp.zeros((DSA_HEADS * seq_pad, 1), F32)
    acc = jnp.zeros((DSA_HEADS * seq_pad, DSA_HD), F32)
    for b in range(nblk):
        e = jnp.exp(lgs[b] - mx)
        den = den + jnp.sum(e, axis=1, keepdims=True)
        acc = acc + times_values(e.astype(BF16), b)
    o = acc / den
    o = jnp.concatenate([o[h * seq_pad:(h + 1) * seq_pad] for h in range(DSA_HEADS)], axis=1)
    o_ref[...] = (o * _silu(g_ref[...])).astype(o_ref.dtype)


def _dsa_sample(p, cache, page_table, n, seq_pad, valid_len):
    npages = page_table.shape[1]
    t = p.shape[0]
    lk = npages * BLK + valid_len
    topk = min(DSA_TOPK_MAX, lk // 4)
    body = functools.partial(_dsa_sample_body, npages=npages, seq_pad=seq_pad, valid_len=valid_len, topk=topk)
    rows = lambda c: pl.BlockSpec((seq_pad, GROUP_W), lambda b, pt: (b, c // 4))
    rows1 = lambda c: pl.BlockSpec((seq_pad, LANE), lambda b, pt: (b, c))
    page_specs = [pl.BlockSpec((1, DSA_CACHE_W, BLK), functools.partial(lambda b, pt, a: (pt[b, a], 0, 0), a=a))
                  for a in range(npages)]
    grid_spec = pltpu.PrefetchScalarGridSpec(
        num_scalar_prefetch=1, grid=(n,),
        in_specs=[rows(_COL['dsa_q']), rows(_COL['idx_q']), rows1(_COL['idx_w']), rows(_COL['dsa_g']),
                  rows1(_COL['dsa_k']), rows1(_COL['dsa_v']), rows1(_COL['idx_k'])] + page_specs,
        out_specs=pl.BlockSpec((seq_pad, GROUP_W), lambda b, pt: (b, 0)))
    return pl.pallas_call(body, grid_spec=grid_spec, out_shape=jax.ShapeDtypeStruct((t, GROUP_W), BF16),
                          name="dsa_sample")(page_table, p, p, p, p, p, p, p, *([cache] * npages))


def _head_sum(x, ones_bd):
    hi = x.astype(BF16)
    lo = (x - hi.astype(F32)).astype(BF16)
    return jnp.dot(hi, ones_bd, preferred_element_type=F32) + jnp.dot(lo, ones_bd, preferred_element_type=F32)


def _softplus(x):
    return jnp.maximum(x, 0.0) + jnp.log1p(jnp.exp(-jnp.abs(x)))


def _rw_prep_body(ps_ref, prev_ref, first_ref, mu_ref, w0_ref, w2_ref, a0_ref, a2_ref, kkw_ref, kaw_ref, ones_ref,
                  r_ref, w_ref, k_ref, v_ref, kk_ref, ka_ref, *, n_seq, seq_len, n_inner):
    rows = n_seq * seq_len
    ps = ps_ref[...]
    xprev = pltpu.roll(ps, 1, 0)
    row = lax.broadcasted_iota(jnp.int32, (rows, SHIFT_P), 0)
    if n_seq == 1:
        c = pl.program_id(0) % n_inner
        first = jnp.where(c == 0, first_ref[0], prev_ref[SUBLANE - 1:SUBLANE, :])
        xprev = jnp.where(row == 0, first, xprev)
    else:
        for s in range(n_seq):
            xprev = jnp.where(row == s * seq_len, first_ref[0, s:s + 1, :], xprev)
    xs = ps + (xprev - ps) * mu_ref[...]
    r = xs[:, 0:GROUP_W]
    k = xs[:, GROUP_W:2 * GROUP_W]
    v = xs[:, 2 * GROUP_W:3 * GROUP_W]
    wl = xs[:, 3 * GROUP_W:3 * GROUP_W + LANE]
    al = xs[:, 3 * GROUP_W + LANE:3 * GROUP_W + 2 * LANE]
    wlin = w0_ref[...] + jnp.dot(jnp.tanh(wl).astype(BF16), w2_ref[...], preferred_element_type=F32)
    w_raw = -_softplus(-wlin) - 0.5
    decay = jnp.exp(-jnp.exp(w_raw))
    a = jax.nn.sigmoid(a0_ref[...] + jnp.dot(al.astype(BF16), a2_ref[...], preferred_element_type=F32))
    kk = k * kkw_ref[...]
    k2 = k * (1.0 + (a - 1.0) * kaw_ref[...])
    nrm = jnp.sqrt(_head_sum(kk * kk, ones_ref[...]))
    kkn = kk / jnp.maximum(nrm, 1e-12)
    r_ref[...] = r
    w_ref[...] = decay
    k_ref[...] = k2
    v_ref[...] = v
    kk_ref[...] = kkn
    ka_ref[...] = kkn * a


def _rw_prep(p, first_prev, prm, *, n_seq, seq_len, n_outer, n_inner):
    rows = n_seq * seq_len
    t = p.shape[0]
    nb = t // rows
    sub_per_blk = rows // SUBLANE
    body = functools.partial(_rw_prep_body, n_seq=n_seq, seq_len=seq_len, n_inner=n_inner)
    c1 = lambda i: (0, 0)
    vec = pl.BlockSpec((1, GROUP_W), c1)
    out = pl.BlockSpec((rows, GROUP_W), lambda i: (i, 0))
    return pl.pallas_call(
        body, grid=(nb,),
        in_specs=[pl.BlockSpec((rows, SHIFT_P), lambda i: (i, _COL['rw_shift'] // 16)),
                  pl.BlockSpec((SUBLANE, SHIFT_P), lambda i: (jnp.maximum(i * sub_per_blk - 1, 0), _COL['rw_shift'] // 16)),
                  pl.BlockSpec((1, n_seq, SHIFT_P), lambda i: (i // n_inner, 0, 0)),
                  pl.BlockSpec((1, SHIFT_P), c1), vec, pl.BlockSpec((LANE, GROUP_W), c1), vec,
                  pl.BlockSpec((LANE, GROUP_W), c1), vec, vec, pl.BlockSpec((GROUP_W, GROUP_W), c1)],
        out_specs=[out] * 6,
        out_shape=[jax.ShapeDtypeStruct((t, GROUP_W), F32)] * 6,
        name="rw_prep")(p, p, first_prev, prm['mu'], prm['w0'], prm['w2'], prm['a0'], prm['a2'],
                        prm['kk'], prm['ka'], prm['ones_bd'])


def _rw_scan_body(w_ref, kk_ref, ka_ref, k_ref, r_ref, v_ref, s0_ref, y_ref, sout_ref, s_ref, *, tb, nj, fold):
    @pl.when(pl.program_id(1) == 0)
    def _():
        s_ref[...] = s0_ref[...]

    si = s_ref.shape[1]
    zero = jnp.zeros((si, LANE), F32)
    keys = range(nj)
    rowj = lambda ref, t, j: ref[t, j:j + 1, :]

    def token_folded(t, carry):
        sa = zero
        for j in keys:
            sa = sa + s_ref[j] * rowj(kk_ref, t, j)
        sa = -(sa + pltpu.roll(sa, LANE // 2, 1))
        vt = v_ref[t]
        for j in keys:
            s_ref[j] = s_ref[j] * rowj(w_ref, t, j) + vt * rowj(k_ref, t, j)
        y = zero
        for j in keys:
            s_new = s_ref[j] + sa * rowj(ka_ref, t, j)
            s_ref[j] = s_new
            y = y + s_new * rowj(r_ref, t, j)
        y_ref[t] = y
        return carry

    def token(t, carry):
        def p1(j, sa):
            return sa + s_ref[j] * kk_ref[t, pl.ds(j, 1), :]
        sa = -lax.fori_loop(0, nj, p1, zero, unroll=8)
        vt = v_ref[t]

        def p2(j, y):
            s_new = (s_ref[j] * w_ref[t, pl.ds(j, 1), :] + sa * ka_ref[t, pl.ds(j, 1), :]
                     + vt * k_ref[t, pl.ds(j, 1), :])
            s_ref[j] = s_new
            return y + s_new * r_ref[t, pl.ds(j, 1), :]
        y_ref[t] = lax.fori_loop(0, nj, p2, zero, unroll=8)
        return carry

    lax.fori_loop(0, tb, token_folded if fold else token, 0)
    sout_ref[...] = s_ref[...]


def _rw_scan(rowvecs, v, s0, *, tb, fold):
    seq, nj, lanes = rowvecs[0].shape
    si = v.shape[1]
    ngroups = lanes // LANE
    body = functools.partial(_rw_scan_body, tb=tb, nj=nj, fold=fold)
    rv = pl.BlockSpec((tb, nj, LANE), lambda g, t: (t, 0, g))
    cv = pl.BlockSpec((tb, si, LANE), lambda g, t: (t, 0, g))
    st = pl.BlockSpec((nj, si, LANE), lambda g, t: (0, 0, g))
    return pl.pallas_call(
        body, grid=(ngroups, seq // tb),
        in_specs=[rv] * 5 + [cv, st],
        out_specs=[cv, st],
        out_shape=[jax.ShapeDtypeStruct((seq, si, lanes), F32), jax.ShapeDtypeStruct((nj, si, lanes), F32)],
        scratch_shapes=[pltpu.VMEM((nj, si, LANE), F32)],
        name="rw_scan")(*rowvecs, v, s0)


def _rw_post_body(y_ref, r_ref, k_ref, v_ref, g_ref, rk_ref, lnw_ref, lnb_ref, ones_ref, o_ref):
    ones_bd = ones_ref[...]
    y = y_ref[...]
    mu = _head_sum(y, ones_bd) * (1.0 / RW_HD)
    d = y - mu
    var = _head_sum(d * d, ones_bd) * (1.0 / RW_HD)
    yn = d * lax.rsqrt(var + RW_LN_EPS) * lnw_ref[...] + lnb_ref[...]
    v = v_ref[...]
    out = yn + _head_sum(r_ref[...] * k_ref[...] * rk_ref[...], ones_bd) * v
    o_ref[...] = (out * _silu(g_ref[...])).astype(o_ref.dtype)


def _rw_post(y, r, k, v, p, prm, rows=BLK):
    t = y.shape[0]
    blk = pl.BlockSpec((rows, GROUP_W), lambda i: (i, 0))
    vec = pl.BlockSpec((1, GROUP_W), lambda i: (0, 0))
    return pl.pallas_call(
        _rw_post_body, grid=(t // rows,),
        in_specs=[blk, blk, blk, blk, pl.BlockSpec((rows, GROUP_W), lambda i: (i, _COL['rw_g'] // 4)),
                  vec, vec, vec, pl.BlockSpec((GROUP_W, GROUP_W), lambda i: (0, 0))],
        out_specs=blk, out_shape=jax.ShapeDtypeStruct((t, GROUP_W), BF16),
        name="rw_post")(y, r, k, v, p, prm['rk'], prm['lnw'], prm['lnb'], prm['ones_bd'])


def _permute_shift(x):
    o1, o2, o3, o4 = GROUP_W, GROUP_W + RW_LORA, 2 * GROUP_W + RW_LORA, 3 * GROUP_W + RW_LORA
    z = lambda n: jnp.zeros(x.shape[:-1] + (n,), x.dtype)
    return jnp.concatenate([x[..., :o1], x[..., o2:o3], x[..., o3:o4], x[..., o1:o2], z(LANE - RW_LORA),
                            x[..., o4:], z(LANE - RW_LORA), z(2 * LANE)], axis=-1)


def _unpermute_shift(x):
    g = GROUP_W
    return jnp.concatenate([x[..., :g], x[..., 3 * g:3 * g + RW_LORA], x[..., g:2 * g], x[..., 2 * g:3 * g],
                            x[..., 3 * g + LANE:3 * g + LANE + RW_LORA]], axis=-1)


def _layout_w_in(w):
    d = w.shape[0]
    off = {}
    o = 0
    for name, n in (('ret', 3 * GROUP_W), ('sb', 4 * GROUP_W), ('dsa_q', GROUP_W), ('dsa_k', DSA_HD), ('dsa_v', DSA_HD),
                    ('idx_q', IDX_HEADS * IDX_HD), ('idx_k', IDX_HD), ('idx_w', IDX_HEADS), ('dsa_g', GROUP_W),
                    ('rw_shift', SHIFT_W), ('rw_g', GROUP_W)):
        off[name] = (o, o + n)
        o += n
    sl = lambda name: w[:, off[name][0]:off[name][1]]
    z = lambda n: jnp.zeros((d, n), w.dtype)
    parts = [sl('ret'), sl('sb'), sl('dsa_q'), sl('idx_q'), sl('dsa_g'), sl('dsa_k'), sl('dsa_v'),
             sl('idx_k'), sl('idx_k'), sl('idx_w'), z(LANE - IDX_HEADS), sl('rw_g'), _permute_shift(sl('rw_shift'))]
    out = jnp.concatenate(parts, axis=1).astype(BF16)
    assert out.shape[1] == N_COLS
    return out


def _pad_lora(w2):
    return jnp.concatenate([w2, jnp.zeros((LANE - RW_LORA, GROUP_W), w2.dtype)], axis=0).astype(BF16)


def _layer_params(l, norm_g, w_in, shift_mu, rw_w0, rw_w2, rw_a0, rw_a2, rw_kk, rw_ka, rw_rk, rw_lnx_w, rw_lnx_b, w_out):
    head = np.arange(GROUP_W) // RW_HD
    row = lambda x: x.reshape(1, -1)
    return dict(g=norm_g[l], w_in=_layout_w_in(w_in[l]), w_out=w_out[l].astype(BF16),
                mu=row(_permute_shift(shift_mu[l])), w0=row(rw_w0[l]), w2=_pad_lora(rw_w2[l]),
                a0=row(rw_a0[l]), a2=_pad_lora(rw_a2[l]), kk=row(rw_kk[l]), ka=row(rw_ka[l]),
                rk=row(rw_rk[l]), lnw=row(rw_lnx_w[l]), lnb=row(rw_lnx_b[l]),
                ones_bd=jnp.asarray(head[:, None] == head[None, :], BF16))


def _scan_prompt(prep, n, seq):
    r, w, k, v, kk, ka = prep
    half = RW_HD // 2
    nh = n * RW_HEADS
    assert 2 * nh == LANE

    def rowvec(x):
        return x.reshape(n, seq, RW_HEADS, 2, half).transpose(1, 4, 3, 0, 2).reshape(seq, half, LANE)

    def colvec(x):
        y = x.reshape(n, seq, RW_HEADS, RW_HD).transpose(1, 3, 0, 2).reshape(seq, RW_HD, nh)
        return jnp.concatenate([y, y], axis=-1)

    s0 = jnp.zeros((half, RW_HD, LANE), F32)
    y, s = _rw_scan([rowvec(w), rowvec(kk), rowvec(ka), rowvec(k), rowvec(r)], colvec(v), s0, tb=32, fold=True)
    y = (y[:, :, :nh] + y[:, :, nh:]).reshape(seq, RW_HD, n, RW_HEADS).transpose(2, 0, 3, 1).reshape(n * seq, GROUP_W)
    s = s.reshape(half, RW_HD, 2, n, RW_HEADS).transpose(3, 4, 1, 2, 0).reshape(n, RW_HEADS, RW_HD, RW_HD)
    return y, s


def _scan_sample(prep, s0, n, seq_pad, valid_len):
    r, w, k, v, kk, ka = prep

    def vec(x):
        return x.reshape(n, seq_pad, RW_HEADS, RW_HD)[:, :valid_len].transpose(1, 3, 0, 2).reshape(
            valid_len, RW_HD, n * RW_HEADS)

    s0 = s0.transpose(3, 2, 0, 1).reshape(RW_HD, RW_HD, n * RW_HEADS)
    y, s = _rw_scan([vec(w), vec(kk), vec(ka), vec(k), vec(r)], vec(v), s0, tb=valid_len, fold=False)
    y = y.reshape(valid_len, RW_HD, n, RW_HEADS).transpose(2, 0, 3, 1)
    y = jnp.pad(y, ((0, 0), (0, seq_pad - valid_len), (0, 0), (0, 0))).reshape(n * seq_pad, GROUP_W)
    s = s.reshape(RW_HD, RW_HD, n, RW_HEADS).transpose(2, 3, 1, 0)
    return y, s


def _prompt_layer(x, h, prm, n, seq, tables, last_g, last_dtype):
    nq = seq // BLK
    p = _in_proj(h, prm['w_in'])
    ret_o, ret_s = _retention(p, jnp.zeros((n, RET_HEADS * RET_DK, RET_DV), F32), tables,
                              n_seq=1, seq_len=BLK, n_outer=n, n_inner=nq, table_by_inner=True)
    sb_o = _sb_prompt(p, n, seq)
    dsa_o = _dsa_prompt(p, n, seq)
    prep = _rw_prep(p, jnp.zeros((n, 1, SHIFT_P), F32), prm, n_seq=1, seq_len=BLK, n_outer=n, n_inner=nq)
    y, wkv_s = _scan_prompt(prep, n, seq)
    rw_o = _rw_post(y, prep[0], prep[2], prep[3], p, prm)
    mix = jnp.concatenate([ret_o, sb_o, dsa_o, rw_o], axis=1)
    x, h = _out_proj(x, mix, prm['w_out'], last_g, last_dtype)
    p3 = p.reshape(n, seq, N_COLS)
    c = lambda name: _COL[name] * LANE
    sb_new = p3[:, :, c('sb_k'):c('sb_g')].reshape(n, seq, 2, SB_HEADS, SB_HD)
    dsa_new = p3[:, :, c('dsa_k'):c('dsa_k') + DSA_CACHE_W]
    shift = _unpermute_shift(p3[:, seq - 1, c('rw_shift'):])
    return x, h, (sb_new, dsa_new, ret_s.reshape(n, RET_HEADS, RET_DK, RET_DV), wkv_s, shift)


def _sample_layer(x, h, prm, n, seq_pad, valid_len, tables, past, page_table, last_g, last_dtype):
    shift_prev, ret_s0, wkv_s0, sb_cache, dsa_cache = past
    n_seq = BLK // seq_pad
    nb = n // n_seq
    p = _in_proj(h, prm['w_in'])
    ret_o, ret_s = _retention(p, ret_s0.reshape(n, RET_HEADS * RET_DK, RET_DV), tables,
                              n_seq=n_seq, seq_len=seq_pad, n_outer=nb, n_inner=1, table_by_inner=False)
    sb_o = _sb_sample(p, sb_cache, page_table, n, seq_pad, valid_len)
    dsa_o = _dsa_sample(p, dsa_cache, page_table, n, seq_pad, valid_len)
    first_prev = _permute_shift(shift_prev).reshape(nb, n_seq, SHIFT_P)
    prep = _rw_prep(p, first_prev, prm, n_seq=n_seq, seq_len=seq_pad, n_outer=nb, n_inner=1)
    y, wkv_s = _scan_sample(prep, wkv_s0, n, seq_pad, valid_len)
    rw_o = _rw_post(y, prep[0], prep[2], prep[3], p, prm)
    mix = jnp.concatenate([ret_o, sb_o, dsa_o, rw_o], axis=1)
    x, h = _out_proj(x, mix, prm['w_out'], last_g, last_dtype)
    p3 = p.reshape(n, seq_pad, N_COLS)[:, :valid_len]
    c = lambda name: _COL[name] * LANE
    sb_new = p3[:, :, c('sb_k'):c('sb_g')].reshape(n, valid_len, 2, SB_HEADS, SB_HD)
    dsa_new = p3[:, :, c('dsa_k'):c('dsa_k') + DSA_CACHE_W]
    shift = _unpermute_shift(p3[:, valid_len - 1, c('rw_shift'):])
    return x, h, (sb_new, dsa_new, ret_s.reshape(n, RET_HEADS, RET_DK, RET_DV), wkv_s, shift)


def kernel(x_prompt, x_sample, cache_sb_kv, cache_dsa, state_ret, state_wkv, state_shift, page_table,
           norm_g, w_in, shift_mu, rw_w0, rw_w2, rw_a0, rw_a2, rw_kk, rw_ka, rw_rk, rw_lnx_w, rw_lnx_b,
           w_out, final_g):
    n_p, seq, d = x_prompt.shape
    n_s, dec_seq, _ = x_sample.shape
    depth = norm_g.shape[0]
    past_len = page_table.shape[1] * cache_sb_kv.shape[2]
    assert cache_sb_kv.shape[2] == BLK and seq % BLK == 0 and d == 4 * GROUP_W
    seq_pad = SUBLANE
    assert dec_seq <= seq_pad and (n_s * seq_pad) % BLK == 0
    n_seq = BLK // seq_pad

    prms = [_layer_params(l, norm_g, w_in, shift_mu, rw_w0, rw_w2, rw_a0, rw_a2, rw_kk, rw_ka, rw_rk,
                          rw_lnx_w, rw_lnx_b, w_out) for l in range(depth)]
    pos = np.arange(seq)
    rows = np.arange(BLK)
    tables_p = _rotary_tables(pos) + _decay_tables(rows, np.zeros(BLK, np.int64), float(BLK))
    tables_s = (_rotary_tables(past_len + rows % seq_pad)
                + _decay_tables(rows % seq_pad, rows // seq_pad, float(dec_seq)))

    n_pool = cache_sb_kv.shape[1]
    sb_pages = cache_sb_kv.reshape(depth * n_pool * SB_PAGE_ROWS, SB_HD)
    dsa_pages = cache_dsa.transpose(0, 1, 3, 2).reshape(depth * n_pool, DSA_CACHE_W, BLK)
    xp = x_prompt.reshape(n_p * seq, d)
    xs = jnp.pad(x_sample, ((0, 0), (0, seq_pad - dec_seq), (0, 0))).reshape(n_s * seq_pad, d)
    hp = _rmsnorm(xp, prms[0]['g'], BF16)
    hs = _rmsnorm(xs, prms[0]['g'], BF16)
    outs_p, outs_s = [], []
    for l in range(depth):
        last = l == depth - 1
        g_next = final_g if last else prms[l + 1]['g']
        dt = F32 if last else BF16
        xp, hp, st = _prompt_layer(xp, hp, prms[l], n_p, seq, tables_p, g_next, dt)
        outs_p.append(st)
        past = (state_shift[l], state_ret[l], state_wkv[l], sb_pages, dsa_pages)
        xs, hs, st = _sample_layer(xs, hs, prms[l], n_s, seq_pad, dec_seq, tables_s, past,
                                   page_table + l * n_pool, g_next, dt)
        outs_s.append(st)
    y_prompt = hp.reshape(n_p, seq, d)
    y_sample = hs.reshape(n_s, seq_pad, d)[:, :dec_seq]
    stack = lambda outs, i: jnp.stack([o[i] for o in outs])
    return (y_prompt, y_sample, *[stack(outs_p, i) for i in range(5)], *[stack(outs_s, i) for i in range(5)])
```

```python
import functools
import math

import jax
import jax.numpy as jnp
import numpy as np
from jax import lax
from jax.experimental import pallas as pl
from jax.experimental.pallas import tpu as pltpu

F32 = jnp.float32
BF16 = jnp.bfloat16
LANE = 128
SUBLANE = 8

GROUP_W = 512
RET_HEADS, RET_DK, RET_DV = 4, 64, 128
SB_HEADS, SB_HD = 4, 128
DSA_HEADS, DSA_HD = 4, 128
IDX_HEADS, IDX_HD = 8, 64
DSA_TOPK_MAX = 256
DSA_CACHE_W = 2 * DSA_HD + IDX_HD
RW_HD, RW_HEADS = 64, 8
RW_LORA = 96
SHIFT_W = 3 * GROUP_W + 2 * RW_LORA
RW_LN_EPS = 64e-5
NORM_EPS = 1e-6
ROPE_BASE = 10000.0
BLK = 128

_COL = dict(ret_q=0, ret_k=2, ret_v=4, ret_g=8, sb_q=12, sb_k=16, sb_v=20, sb_g=24,
            dsa_q=28, idx_q=32, dsa_g=36, dsa_k=40, dsa_v=41, idx_k=42, idx_w=43,
            rw_g=44, rw_shift=48)
N_COLS = 64 * LANE
SHIFT_P = 16 * LANE
NT = (((1,), (1,)), ((), ()))
TN = (((0,), (0,)), ((), ()))
INT_MIN = -2 ** 31


def _silu(x):
    return x * jax.nn.sigmoid(x)


def _rmsnorm_body(x_ref, g_ref, o_ref):
    x = x_ref[...]
    y = x * lax.rsqrt(jnp.mean(x * x, axis=-1, keepdims=True) + NORM_EPS)
    o_ref[...] = (y * g_ref[...]).astype(o_ref.dtype)


def _rmsnorm(x, g, out_dtype, tm=256):
    t, d = x.shape
    tm = min(tm, t)
    return pl.pallas_call(
        _rmsnorm_body, grid=(t // tm,),
        in_specs=[pl.BlockSpec((tm, d), lambda i: (i, 0)), pl.BlockSpec((1, d), lambda i: (0, 0))],
        out_specs=pl.BlockSpec((tm, d), lambda i: (i, 0)),
        out_shape=jax.ShapeDtypeStruct((t, d), out_dtype), name="rmsnorm")(x, g.reshape(1, d))


def _matmul_body(a_ref, b_ref, o_ref):
    o_ref[...] = jnp.dot(a_ref[...], b_ref[...], preferred_element_type=F32)


def _in_proj(h, w, tm=512, tn=2048):
    t, d = h.shape
    nc = w.shape[1]
    tm = min(tm, t)
    return pl.pallas_call(
        _matmul_body, grid=(t // tm, nc // tn),
        in_specs=[pl.BlockSpec((tm, d), lambda i, j: (i, 0)), pl.BlockSpec((d, tn), lambda i, j: (0, j))],
        out_specs=pl.BlockSpec((tm, tn), lambda i, j: (i, j)),
        out_shape=jax.ShapeDtypeStruct((t, nc), F32), name="in_proj")(h, w)


def _out_proj_body(x_ref, m_ref, w_ref, g_ref, xo_ref, yo_ref):
    xn = x_ref[...] + jnp.dot(m_ref[...], w_ref[...], preferred_element_type=F32)
    xo_ref[...] = xn
    y = xn * lax.rsqrt(jnp.mean(xn * xn, axis=-1, keepdims=True) + NORM_EPS)
    yo_ref[...] = (y * g_ref[...]).astype(yo_ref.dtype)


def _out_proj(x, mix, w, g, y_dtype, tm=256):
    t, d = x.shape
    tm = min(tm, t)
    return pl.pallas_call(
        _out_proj_body, grid=(t // tm,),
        in_specs=[pl.BlockSpec((tm, d), lambda i: (i, 0)), pl.BlockSpec((tm, d), lambda i: (i, 0)),
                  pl.BlockSpec((d, d), lambda i: (0, 0)), pl.BlockSpec((1, d), lambda i: (0, 0))],
        out_specs=[pl.BlockSpec((tm, d), lambda i: (i, 0)), pl.BlockSpec((tm, d), lambda i: (i, 0))],
        out_shape=[jax.ShapeDtypeStruct((t, d), F32), jax.ShapeDtypeStruct((t, d), y_dtype)],
        name="out_proj")(x, mix, w, g.reshape(1, d))


def _ret_body(q_ref, k_ref, v_ref, g_ref, cos_ref, sin_ref, dmask_ref, qdec_ref, kdec_ref, sdec_ref,
              s0_ref, o_ref, sout_ref, ss_ref, *, n_seq, seq_len):
    rows = n_seq * seq_len

    @pl.when(pl.program_id(1) == 0)
    def _():
        ss_ref[...] = s0_ref[...]

    lane = lax.broadcasted_iota(jnp.int32, (rows, RET_HEADS * RET_DK), 1)
    first_half = (lane % RET_DK) < (RET_DK // 2)
    cosf, sinf = cos_ref[...], sin_ref[...]
    width = RET_HEADS * RET_DK

    def rot(x):
        swapped = jnp.where(first_half, pltpu.roll(x, width - RET_DK // 2, 1), pltpu.roll(x, RET_DK // 2, 1))
        return x * cosf + swapped * sinf

    qr = rot(q_ref[...])
    kr = rot(k_ref[...]) * (RET_DK ** -0.5)
    kr_b = kr.astype(BF16)
    kd = kr * kdec_ref[...]
    v = v_ref[...]
    g = g_ref[...]
    qdec = qdec_ref[...]
    upd = [jnp.zeros((width, RET_DV), F32) for _ in range(n_seq)]
    for h in range(RET_HEADS):
        mh = (lane // RET_DK) == h
        qh = jnp.where(mh, qr, 0.0).astype(BF16)
        att = lax.dot_general(qh, kr_b, NT, preferred_element_type=F32) * dmask_ref[h]
        vh = v[:, h * RET_DV:(h + 1) * RET_DV].astype(BF16)
        o = jnp.dot(att.astype(BF16), vh, preferred_element_type=F32)
        inter = [jnp.dot(qh[s * seq_len:(s + 1) * seq_len], ss_ref[s].astype(BF16), preferred_element_type=F32)
                 for s in range(n_seq)]
        inter = inter[0] if n_seq == 1 else jnp.concatenate(inter, axis=0)
        o = o + inter * qdec[:, h * RET_DV:(h + 1) * RET_DV]
        o = o * lax.rsqrt(jnp.mean(o * o, axis=-1, keepdims=True) + NORM_EPS)
        o_ref[:, h * RET_DV:(h + 1) * RET_DV] = (o * _silu(g[:, h * RET_DV:(h + 1) * RET_DV])).astype(o_ref.dtype)
        kdh = jnp.where(mh, kd, 0.0).astype(BF16)
        for s in range(n_seq):
            upd[s] = upd[s] + lax.dot_general(kdh[s * seq_len:(s + 1) * seq_len], vh[s * seq_len:(s + 1) * seq_len],
                                              TN, preferred_element_type=F32)
    sdec = sdec_ref[...]
    for s in range(n_seq):
        new = ss_ref[s] * sdec + upd[s]
        ss_ref[s] = new
        sout_ref[s] = new


def _rotary_tables(pos_abs):
    half = RET_DK // 2
    inv = 1.0 / (ROPE_BASE ** (jnp.arange(half, dtype=F32) / half))
    ang = jnp.asarray(pos_abs, F32)[:, None] * inv[None, :]
    cos, sin = jnp.cos(ang), jnp.sin(ang)
    cosf = jnp.tile(jnp.concatenate([cos, cos], axis=1), (1, RET_HEADS))
    sinf = jnp.tile(jnp.concatenate([-sin, sin], axis=1), (1, RET_HEADS))
    return cosf, sinf


def _decay_tables(pos_in_seq, seq_id, valid_len):
    log_g = jnp.log1p(-jnp.exp2(-5.0 - jnp.arange(RET_HEADS, dtype=F32)))
    i = jnp.asarray(pos_in_seq, F32)
    sid = jnp.asarray(seq_id)
    diff = i[:, None] - i[None, :]
    same = sid[:, None] == sid[None, :]
    dmask = jnp.where((diff >= 0)[None] & same[None],
                      jnp.exp(jnp.maximum(diff, 0.0)[None] * log_g[:, None, None]), 0.0)
    q_dec = jnp.exp((i + 1.0)[:, None] * log_g[None, :])
    k_dec = jnp.where((i < valid_len)[:, None], jnp.exp((valid_len - 1.0 - i)[:, None] * log_g[None, :]), 0.0)
    s_dec = jnp.exp(valid_len * log_g)
    q_dec = jnp.repeat(q_dec, RET_DV, axis=1)
    k_dec = jnp.repeat(k_dec, RET_DK, axis=1)
    s_dec = jnp.broadcast_to(jnp.repeat(s_dec, RET_DK)[:, None], (RET_HEADS * RET_DK, RET_DV))
    return dmask, q_dec, k_dec, s_dec


def _retention(p, s0, tables, *, n_seq, seq_len, n_outer, n_inner, table_by_inner):
    rows = n_seq * seq_len
    cosf, sinf, dmask, q_dec, k_dec, s_dec = tables
    t = p.shape[0]
    width = RET_HEADS * RET_DK
    rb = lambda o, c: o * n_inner + c
    tb = (lambda o, c: (c, 0)) if table_by_inner else (lambda o, c: (0, 0))
    const2 = lambda o, c: (0, 0)
    body = functools.partial(_ret_body, n_seq=n_seq, seq_len=seq_len)
    return pl.pallas_call(
        body, grid=(n_outer, n_inner),
        in_specs=[pl.BlockSpec((rows, width), lambda o, c: (rb(o, c), _COL['ret_q'] // 2)),
                  pl.BlockSpec((rows, width), lambda o, c: (rb(o, c), _COL['ret_k'] // 2)),
                  pl.BlockSpec((rows, GROUP_W), lambda o, c: (rb(o, c), _COL['ret_v'] // 4)),
                  pl.BlockSpec((rows, GROUP_W), lambda o, c: (rb(o, c), _COL['ret_g'] // 4)),
                  pl.BlockSpec((rows, width), tb), pl.BlockSpec((rows, width), tb),
                  pl.BlockSpec((RET_HEADS, rows, rows), lambda o, c: (0, 0, 0)),
                  pl.BlockSpec((rows, GROUP_W), const2), pl.BlockSpec((rows, width), const2),
                  pl.BlockSpec((width, RET_DV), const2),
                  pl.BlockSpec((n_seq, width, RET_DV), lambda o, c: (o, 0, 0))],
        out_specs=[pl.BlockSpec((rows, GROUP_W), lambda o, c: (rb(o, c), 0)),
                   pl.BlockSpec((n_seq, width, RET_DV), lambda o, c: (o, 0, 0))],
        out_shape=[jax.ShapeDtypeStruct((t, GROUP_W), BF16),
                   jax.ShapeDtypeStruct(s0.shape, F32)],
        scratch_shapes=[pltpu.VMEM((n_seq, width, RET_DV), F32)],
        name="retention")(p, p, p, p, cosf, sinf, dmask, q_dec, k_dec, s_dec, s0)


def _sb_scores(z, vis, run, tri):
    lp = jnp.log1p(jnp.exp(-jnp.abs(z)))
    ls = jnp.minimum(z, 0.0) - lp
    lm = jnp.minimum(-z, 0.0) - lp
    l = jnp.where(vis, lm, 0.0)
    l_hi = l.astype(BF16)
    l_lo = (l - l_hi.astype(F32)).astype(BF16)
    after = run + jnp.dot(l_hi, tri, preferred_element_type=F32) + jnp.dot(l_lo, tri, preferred_element_type=F32)
    w = jnp.where(vis, jnp.exp(ls + after), 0.0)
    return w, run + jnp.sum(l, axis=1, keepdims=True)


SB_DEAD = -104.0


def _sb_live(run):
    return (jnp.max(run) >= SB_DEAD).astype(jnp.int32)


def _strict_lower(n):
    r = lax.broadcasted_iota(jnp.int32, (n, n), 0)
    c = lax.broadcasted_iota(jnp.int32, (n, n), 1)
    return (r > c).astype(BF16)


def _sb_prompt_body(q_ref, k_ref, v_ref, g_ref, o_ref):
    i = pl.program_id(1)
    heads = range(SB_HEADS)
    head = lambda h: slice(h * SB_HD, (h + 1) * SB_HD)
    q = [q_ref[:, head(h)].astype(BF16) for h in heads]
    scale = SB_HD ** -0.5
    row = lax.broadcasted_iota(jnp.int32, (BLK, BLK), 0)
    col = lax.broadcasted_iota(jnp.int32, (BLK, BLK), 1)
    tri = (row > col).astype(BF16)

    def cond(carry):
        jj, _, _, live = carry
        return (jj <= i) & (live > 0)

    def step(carry):
        jj, acc, run, _ = carry
        j = i - jj
        start = pl.multiple_of(j * BLK, BLK)
        vis = (j < i) | (col < row)
        acc, run = list(acc), list(run)
        for h in heads:
            kj = k_ref[pl.ds(start, BLK), head(h)].astype(BF16)
            vj = v_ref[pl.ds(start, BLK), head(h)].astype(BF16)
            z = lax.dot_general(q[h], kj, NT, preferred_element_type=F32) * scale
            w, run[h] = _sb_scores(z, vis, run[h], tri)
            acc[h] = acc[h] + jnp.dot(w.astype(BF16), vj, preferred_element_type=F32)
        live = _sb_live(run[0])
        for h in heads[1:]:
            live = jnp.maximum(live, _sb_live(run[h]))
        return jj + 1, tuple(acc), tuple(run), live

    init = (jnp.int32(0), tuple(jnp.zeros((BLK, SB_HD), F32) for _ in heads),
            tuple(jnp.zeros((BLK, 1), F32) for _ in heads), jnp.int32(1))
    _, acc, _, _ = lax.while_loop(cond, step, init)
    for h in heads:
        o_ref[:, head(h)] = (acc[h] * _silu(g_ref[:, head(h)])).astype(o_ref.dtype)


def _sb_prompt(p, n, seq):
    nq = seq // BLK
    t = p.shape[0]
    rows = lambda c: pl.BlockSpec((BLK, GROUP_W), lambda b, i: (b * nq + i, c // 4))
    keys = lambda c: pl.BlockSpec((seq, GROUP_W), lambda b, i: (b, c // 4))
    return pl.pallas_call(
        _sb_prompt_body, grid=(n, nq),
        in_specs=[rows(_COL['sb_q']), keys(_COL['sb_k']), keys(_COL['sb_v']), rows(_COL['sb_g'])],
        out_specs=pl.BlockSpec((BLK, GROUP_W), lambda b, i: (b * nq + i, 0)),
        out_shape=jax.ShapeDtypeStruct((t, GROUP_W), BF16), name="sb_prompt")(p, p, p, p)


SB_PAGE_ROWS = BLK * 2 * SB_HEADS
SB_AHEAD = 2


def _sb_sample_body(pt_ref, q_ref, kn_ref, vn_ref, g_ref, cache_ref, o_ref, buf_ref, sem_ref, *,
                    seq_pad, valid_len, npages):
    b = pl.program_id(0)
    m = SB_HEADS * seq_pad
    scale = SB_HD ** -0.5
    tri = _strict_lower(BLK)
    q = q_ref[...]
    heads = range(SB_HEADS)
    head_cols = lambda x, h: x[:, h * SB_HD:(h + 1) * SB_HD]
    qh = [head_cols(q, h).astype(BF16) for h in heads]

    demand_slot = 2 * SB_AHEAD

    def page_copy(seq, j, slot):
        page = pt_ref[seq, npages - 1 - j]
        src = cache_ref.at[pl.ds(pl.multiple_of(page * SB_PAGE_ROWS, SB_PAGE_ROWS), SB_PAGE_ROWS), :]
        return pltpu.make_async_copy(src, buf_ref.at[slot], sem_ref.at[slot])

    ahead_slot = lambda seq, j: (seq % 2) * SB_AHEAD + j

    @pl.when(b == 0)
    def _():
        for a in range(SB_AHEAD):
            page_copy(b, a, ahead_slot(b, a)).start()

    @pl.when(b + 1 < pl.num_programs(0))
    def _():
        for a in range(SB_AHEAD):
            page_copy(b + 1, a, ahead_slot(b + 1, a)).start()

    def attend(kh, vh, vis, acc, run):
        z = jnp.concatenate([lax.dot_general(qh[h], kh[h].astype(BF16), NT, preferred_element_type=F32)
                             for h in heads], axis=0) * scale
        w, run = _sb_scores(z, vis, run, tri)
        w = w.astype(BF16)
        upd = jnp.concatenate([jnp.dot(w[h * seq_pad:(h + 1) * seq_pad], vh[h].astype(BF16),
                                       preferred_element_type=F32) for h in heads], axis=0)
        return acc + upd, run

    pad = jnp.zeros((BLK - seq_pad, GROUP_W), F32)
    kn = jnp.concatenate([kn_ref[...], pad], axis=0)
    vn = jnp.concatenate([vn_ref[...], pad], axis=0)
    tq = lax.broadcasted_iota(jnp.int32, (m, BLK), 0) % seq_pad
    tk = lax.broadcasted_iota(jnp.int32, (m, BLK), 1)
    acc, run = attend([head_cols(kn, h) for h in heads], [head_cols(vn, h) for h in heads],
                      (tk < tq) & (tk < valid_len), jnp.zeros((m, SB_HD), F32), jnp.zeros((m, 1), F32))
    all_visible = jnp.full((m, BLK), True)

    def cond(carry):
        j, _, _, live = carry
        return (j < npages) & (live > 0)

    def step(carry):
        j, acc, run, _ = carry

        @pl.when(j < SB_AHEAD)
        def _():
            page_copy(b, j, ahead_slot(b, j)).wait()

        @pl.when(j >= SB_AHEAD)
        def _():
            copy = page_copy(b, j, demand_slot)
            copy.start()
            copy.wait()

        slot = jnp.where(j < SB_AHEAD, ahead_slot(b, j), demand_slot)
        kh = [buf_ref[slot, pl.ds(h, BLK, stride=2 * SB_HEADS), :] for h in heads]
        vh = [buf_ref[slot, pl.ds(SB_HEADS + h, BLK, stride=2 * SB_HEADS), :] for h in heads]
        acc, run = attend(kh, vh, all_visible, acc, run)
        return j + 1, acc, run, _sb_live(run)

    j, acc, _, _ = lax.while_loop(cond, step, (jnp.int32(0), acc, run, _sb_live(run)))

    for a in range(SB_AHEAD):
        @pl.when(j <= a)
        def _(a=a):
            page_copy(b, a, ahead_slot(b, a)).wait()

    out = jnp.concatenate([acc[h * seq_pad:(h + 1) * seq_pad] for h in heads], axis=1)
    o_ref[...] = (out * _silu(g_ref[...])).astype(o_ref.dtype)


def _sb_sample(p, cache, page_table, n, seq_pad, valid_len):
    npages = page_table.shape[1]
    assert npages >= SB_AHEAD
    t = p.shape[0]
    body = functools.partial(_sb_sample_body, seq_pad=seq_pad, valid_len=valid_len, npages=npages)
    rows = lambda c: pl.BlockSpec((seq_pad, GROUP_W), lambda b, pt: (b, c // 4))
    grid_spec = pltpu.PrefetchScalarGridSpec(
        num_scalar_prefetch=1, grid=(n,),
        in_specs=[rows(_COL['sb_q']), rows(_COL['sb_k']), rows(_COL['sb_v']), rows(_COL['sb_g']),
                  pl.BlockSpec(memory_space=pl.ANY)],
        out_specs=pl.BlockSpec((seq_pad, GROUP_W), lambda b, pt: (b, 0)),
        scratch_shapes=[pltpu.VMEM((2 * SB_AHEAD + 1, SB_PAGE_ROWS, SB_HD), F32),
                        pltpu.SemaphoreType.DMA((2 * SB_AHEAD + 1,))])
    return pl.pallas_call(body, grid_spec=grid_spec, out_shape=jax.ShapeDtypeStruct((t, GROUP_W), BF16),
                          name="sb_sample")(page_table, p, p, p, p, cache)


HALF_BIAS = 2 ** 15


def _kth_key_halves(key, topk):
    rows, width = key.shape
    hi = (key >> 16).astype(jnp.int16)
    lo = ((key & 0xffff) - HALF_BIAS).astype(jnp.int16)

    def count(mask):
        ones = jnp.where(mask, jnp.int16(1), jnp.int16(0))
        part = ones[:, 0:LANE]
        for c in range(1, width // LANE):
            part = part + ones[:, c * LANE:(c + 1) * LANE]
        return jnp.sum(part.astype(jnp.int32).astype(F32), axis=1, keepdims=True)

    def search(x, need):
        def step(s, thr):
            shift = 14 - 2 * s
            n_ok = jnp.zeros((rows, 1), jnp.int32)
            for c in (1, 2, 3):
                cand = (thr | jnp.left_shift(jnp.int32(c), shift)) - HALF_BIAS
                n_ok = n_ok + (count(x >= cand.astype(jnp.int16)) >= need).astype(jnp.int32)
            return thr | jnp.left_shift(n_ok, shift)
        return lax.fori_loop(0, 8, step, jnp.zeros((rows, 1), jnp.int32)) - HALF_BIAS

    t_hi = search(hi, topk)
    t_hi16 = t_hi.astype(jnp.int16)
    above = count(hi > t_hi16)
    t_lo = search(jnp.where(hi == t_hi16, lo, jnp.int16(-HALF_BIAS)), topk - above)
    return jnp.left_shift(t_hi, 16) | (t_lo + HALF_BIAS)


def _kth_key_bits(key, few, topk, radix_bits):
    rows = key.shape[0]
    int_min = jnp.int32(INT_MIN)
    n_steps = 32 // radix_bits

    def cond(carry):
        step, _, _, pending = carry
        return (step < n_steps) & (pending > 0)

    def search(carry):
        step, thr, open_rows, _ = carry
        shift = 32 - radix_bits * (step + 1)
        n_ok = jnp.zeros((rows, 1), jnp.int32)
        for c in range(1, 2 ** radix_bits):
            cand = thr | jnp.left_shift(jnp.int32(c), shift)
            cnt = jnp.sum((key >= (cand ^ int_min)).astype(F32), axis=1, keepdims=True)
            n_ok = n_ok + (cnt >= topk).astype(jnp.int32)
            open_rows = jnp.where(cnt == topk, 0, open_rows)
        thr = thr | jnp.left_shift(n_ok, shift)
        return step + 1, thr, open_rows, jnp.max(open_rows)

    open_rows = jnp.where(few, 0, 1)
    init = (jnp.int32(0), jnp.zeros((rows, 1), jnp.int32), open_rows, jnp.max(open_rows))
    _, thr, _, _ = lax.while_loop(cond, search, init)
    return thr ^ int_min


def _topk_select(score, vis, topk, radix_bits=None):
    rows, width = score.shape
    score = jnp.where(vis, score, -jnp.inf)
    score = jnp.where(score == 0.0, 0.0, score)
    u = pltpu.bitcast(score, jnp.int32)
    key = u ^ ((u >> 31) & 0x7fffffff)
    if radix_bits is None:
        thr = _kth_key_halves(key, topk)
    else:
        few = jnp.sum(vis.astype(F32), axis=1, keepdims=True) <= topk
        thr = _kth_key_bits(key, few, topk, radix_bits)
    gt = key > thr
    eq = key == thr
    need = topk - jnp.sum(gt.astype(F32), axis=1, keepdims=True)
    r = lax.broadcasted_iota(jnp.int32, (BLK, BLK), 0)
    c = lax.broadcasted_iota(jnp.int32, (BLK, BLK), 1)
    before = (r < c).astype(BF16)
    seen = jnp.zeros((rows, 1), F32)
    pieces = []
    for b in range(width // BLK):
        eq_b = eq[:, b * BLK:(b + 1) * BLK]
        eq_f = eq_b.astype(BF16)
        rank = seen + jnp.dot(eq_f, before, preferred_element_type=F32)
        pieces.append(gt[:, b * BLK:(b + 1) * BLK] | (eq_b & (rank < need)))
        seen = seen + jnp.sum(eq_f.astype(F32), axis=1, keepdims=True)
    return jnp.concatenate(pieces, axis=1) & vis


def _dsa_prompt_body(q_ref, iq_ref, iw_ref, g_ref, k_ref, v_ref, ik_ref, o_ref, *, topk, widths):
    i = pl.program_id(1)

    def compute(width):
        ikk = ik_ref[0:width, :].astype(BF16)
        lane = lax.broadcasted_iota(jnp.int32, (BLK, LANE), 1)
        iw = iw_ref[...] * (IDX_HEADS ** -0.5)
        score = jnp.zeros((BLK, width), F32)
        for h in range(IDX_HEADS):
            pair = iq_ref[:, (h // 2) * LANE:(h // 2 + 1) * LANE]
            iqh = jnp.where((lane // IDX_HD) == (h % 2), pair, 0.0).astype(BF16)
            rel = jnp.maximum(lax.dot_general(iqh, ikk, NT, preferred_element_type=F32), 0.0) * (IDX_HD ** -0.5)
            score = score + rel * iw[:, h:h + 1]
        qpos = i * BLK + lax.broadcasted_iota(jnp.int32, (BLK, width), 0)
        kpos = lax.broadcasted_iota(jnp.int32, (BLK, width), 1)
        sel = _topk_select(score, kpos <= qpos, topk)
        kb = k_ref[0:width, :].astype(BF16)
        vb = v_ref[0:width, :].astype(BF16)
        g = g_ref[...]
        for h in range(DSA_HEADS):
            qh = q_ref[:, h * DSA_HD:(h + 1) * DSA_HD].astype(BF16)
            lg = lax.dot_general(qh, kb, NT, preferred_element_type=F32) * (DSA_HD ** -0.5)
            lg = jnp.where(sel, lg, -jnp.inf)
            e = jnp.exp(lg - jnp.max(lg, axis=1, keepdims=True))
            o = jnp.dot(e.astype(BF16), vb, preferred_element_type=F32) / jnp.sum(e, axis=1, keepdims=True)
            o_ref[:, h * DSA_HD:(h + 1) * DSA_HD] = (o * _silu(g[:, h * DSA_HD:(h + 1) * DSA_HD])).astype(o_ref.dtype)

    per = widths[0] // BLK
    for v, width in enumerate(widths):
        pl.when(i // per == v)(functools.partial(compute, width))


def _dsa_prompt(p, n, seq):
    nq = seq // BLK
    t = p.shape[0]
    topk = min(DSA_TOPK_MAX, seq // 4)
    nvar = math.gcd(nq, 8)
    widths = tuple((v + 1) * (nq // nvar) * BLK for v in range(nvar))
    body = functools.partial(_dsa_prompt_body, topk=topk, widths=widths)
    rowblk = lambda b, i: b * nq + i
    return pl.pallas_call(
        body, grid=(n, nq),
        in_specs=[pl.BlockSpec((BLK, GROUP_W), lambda b, i: (rowblk(b, i), _COL['dsa_q'] // 4)),
                  pl.BlockSpec((BLK, GROUP_W), lambda b, i: (rowblk(b, i), _COL['idx_q'] // 4)),
                  pl.BlockSpec((BLK, LANE), lambda b, i: (rowblk(b, i), _COL['idx_w'])),
                  pl.BlockSpec((BLK, GROUP_W), lambda b, i: (rowblk(b, i), _COL['dsa_g'] // 4)),
                  pl.BlockSpec((seq, DSA_HD), lambda b, i: (b, _COL['dsa_k'])),
                  pl.BlockSpec((seq, DSA_HD), lambda b, i: (b, _COL['dsa_v'])),
                  pl.BlockSpec((seq, LANE), lambda b, i: (b, _COL['idx_k']))],
        out_specs=pl.BlockSpec((BLK, GROUP_W), lambda b, i: (rowblk(b, i), 0)),
        out_shape=jax.ShapeDtypeStruct((t, GROUP_W), BF16), name="dsa_prompt")(p, p, p, p, p, p, p)


def _dsa_sample_body(pt_ref, q_ref, iq_ref, iw_ref, g_ref, kn_ref, vn_ref, ikn_ref, *rest,
                     npages, seq_pad, valid_len, topk):
    pages = rest[:npages]
    o_ref = rest[npages]
    nblk = npages + 1
    iq = iq_ref[...]
    iqs = jnp.concatenate([iq[:, h * IDX_HD:(h + 1) * IDX_HD] for h in range(IDX_HEADS)], axis=0).astype(BF16)
    iw = iw_ref[...] * (IDX_HEADS ** -0.5)
    zpad = lambda x: jnp.concatenate([x, jnp.zeros((BLK - seq_pad, x.shape[1]), F32)], axis=0).astype(BF16)

    def against_keys(x, b, lo, hi, new_ref):
        if b < npages:
            return jnp.dot(x, pages[b][0, lo:hi, :].astype(BF16), preferred_element_type=F32)
        return lax.dot_general(x, zpad(new_ref[:, 0:hi - lo]), NT, preferred_element_type=F32)

    def times_values(e, b):
        if b < npages:
            return lax.dot_general(e, pages[b][0, DSA_HD:2 * DSA_HD, :].astype(BF16), NT, preferred_element_type=F32)
        return jnp.dot(e, zpad(vn_ref[...]), preferred_element_type=F32)

    pieces = []
    for b in range(nblk):
        s = against_keys(iqs, b, 2 * DSA_HD, DSA_CACHE_W, ikn_ref)
        rel = jnp.maximum(s, 0.0) * (IDX_HD ** -0.5)
        sc = jnp.zeros((seq_pad, BLK), F32)
        for h in range(IDX_HEADS):
            sc = sc + rel[h * seq_pad:(h + 1) * seq_pad] * iw[:, h:h + 1]
        pieces.append(sc)
    score = jnp.concatenate(pieces, axis=1)
    tq = lax.broadcasted_iota(jnp.int32, (seq_pad, nblk * BLK), 0)
    kk = lax.broadcasted_iota(jnp.int32, (seq_pad, nblk * BLK), 1)
    tnew = kk - npages * BLK
    vis = (tnew < 0) | ((tnew <= tq) & (tnew < valid_len))
    sel = _topk_select(score, vis, topk, radix_bits=4)
    q = q_ref[...]
    q4 = jnp.concatenate([q[:, h * DSA_HD:(h + 1) * DSA_HD] for h in range(DSA_HEADS)], axis=0).astype(BF16)
    lgs = []
    for b in range(nblk):
        lg = against_keys(q4, b, 0, DSA_HD, kn_ref) * (DSA_HD ** -0.5)
        sb = sel[:, b * BLK:(b + 1) * BLK]
        lgs.append(jnp.where(jnp.concatenate([sb] * DSA_HEADS, axis=0), lg, -jnp.inf))
    mx = lgs[0].max(axis=1, keepdims=True)
    for lg in lgs[1:]:
        mx = jnp.maximum(mx, lg.max(axis=1, keepdims=True))
    den = jnp.zeros((DSA_HEADS * seq_pad, 1), F32)
    acc = jnp.zeros((DSA_HEADS * seq_pad, DSA_HD), F32)
    for b in range(nblk):
        e = jnp.exp(lgs[b] - mx)
        den = den + jnp.sum(e, axis=1, keepdims=True)
        acc = acc + times_values(e.astype(BF16), b)
    o = acc / den
    o = jnp.concatenate([o[h * seq_pad:(h + 1) * seq_pad] for h in range(DSA_HEADS)], axis=1)
    o_ref[...] = (o * _silu(g_ref[...])).astype(o_ref.dtype)


def _dsa_sample(p, cache, page_table, n, seq_pad, valid_len):
    npages = page_table.shape[1]
    t = p.shape[0]
    lk = npages * BLK + valid_len
    topk = min(DSA_TOPK_MAX, lk // 4)
    body = functools.partial(_dsa_sample_body, npages=npages, seq_pad=seq_pad, valid_len=valid_len, topk=topk)
    rows = lambda c: pl.BlockSpec((seq_pad, GROUP_W), lambda b, pt: (b, c // 4))
    rows1 = lambda c: pl.BlockSpec((seq_pad, LANE), lambda b, pt: (b, c))
    page_specs = [pl.BlockSpec((1, DSA_CACHE_W, BLK), functools.partial(lambda b, pt, a: (pt[b, a], 0, 0), a=a))
                  for a in range(npages)]
    grid_spec = pltpu.PrefetchScalarGridSpec(
        num_scalar_prefetch=1, grid=(n,),
        in_specs=[rows(_COL['dsa_q']), rows(_COL['idx_q']), rows1(_COL['idx_w']), rows(_COL['dsa_g']),
                  rows1(_COL['dsa_k']), rows1(_COL['dsa_v']), rows1(_COL['idx_k'])] + page_specs,
        out_specs=pl.BlockSpec((seq_pad, GROUP_W), lambda b, pt: (b, 0)))
    return pl.pallas_call(body, grid_spec=grid_spec, out_shape=jax.ShapeDtypeStruct((t, GROUP_W), BF16),
                          name="dsa_sample")(page_table, p, p, p, p, p, p, p, *([cache] * npages))


def _head_sum(x, ones_bd):
    hi = x.astype(BF16)
    lo = (x - hi.astype(F32)).astype(BF16)
    return jnp.dot(hi, ones_bd, preferred_element_type=F32) + jnp.dot(lo, ones_bd, preferred_element_type=F32)


def _softplus(x):
    return jnp.maximum(x, 0.0) + jnp.log1p(jnp.exp(-jnp.abs(x)))


def _rw_prep_body(ps_ref, prev_ref, first_ref, mu_ref, w0_ref, w2_ref, a0_ref, a2_ref, kkw_ref, kaw_ref, ones_ref,
                  r_ref, w_ref, k_ref, v_ref, kk_ref, ka_ref, *, n_seq, seq_len, n_inner):
    rows = n_seq * seq_len
    ps = ps_ref[...]
    xprev = pltpu.roll(ps, 1, 0)
    row = lax.broadcasted_iota(jnp.int32, (rows, SHIFT_P), 0)
    if n_seq == 1:
        c = pl.program_id(0) % n_inner
        first = jnp.where(c == 0, first_ref[0], prev_ref[SUBLANE - 1:SUBLANE, :])
        xprev = jnp.where(row == 0, first, xprev)
    else:
        for s in range(n_seq):
            xprev = jnp.where(row == s * seq_len, first_ref[0, s:s + 1, :], xprev)
    xs = ps + (xprev - ps) * mu_ref[...]
    r = xs[:, 0:GROUP_W]
    k = xs[:, GROUP_W:2 * GROUP_W]
    v = xs[:, 2 * GROUP_W:3 * GROUP_W]
    wl = xs[:, 3 * GROUP_W:3 * GROUP_W + LANE]
    al = xs[:, 3 * GROUP_W + LANE:3 * GROUP_W + 2 * LANE]
    wlin = w0_ref[...] + jnp.dot(jnp.tanh(wl).astype(BF16), w2_ref[...], preferred_element_type=F32)
    w_raw = -_softplus(-wlin) - 0.5
    decay = jnp.exp(-jnp.exp(w_raw))
    a = jax.nn.sigmoid(a0_ref[...] + jnp.dot(al.astype(BF16), a2_ref[...], preferred_element_type=F32))
    kk = k * kkw_ref[...]
    k2 = k * (1.0 + (a - 1.0) * kaw_ref[...])
    nrm = jnp.sqrt(_head_sum(kk * kk, ones_ref[...]))
    kkn = kk / jnp.maximum(nrm, 1e-12)
    r_ref[...] = r
    w_ref[...] = decay
    k_ref[...] = k2
    v_ref[...] = v
    kk_ref[...] = kkn
    ka_ref[...] = kkn * a


def _rw_prep(p, first_prev, prm, *, n_seq, seq_len, n_outer, n_inner):
    rows = n_seq * seq_len
    t = p.shape[0]
    nb = t // rows
    sub_per_blk = rows // SUBLANE
    body = functools.partial(_rw_prep_body, n_seq=n_seq, seq_len=seq_len, n_inner=n_inner)
    c1 = lambda i: (0, 0)
    vec = pl.BlockSpec((1, GROUP_W), c1)
    out = pl.BlockSpec((rows, GROUP_W), lambda i: (i, 0))
    return pl.pallas_call(
        body, grid=(nb,),
        in_specs=[pl.BlockSpec((rows, SHIFT_P), lambda i: (i, _COL['rw_shift'] // 16)),
                  pl.BlockSpec((SUBLANE, SHIFT_P), lambda i: (jnp.maximum(i * sub_per_blk - 1, 0), _COL['rw_shift'] // 16)),
                  pl.BlockSpec((1, n_seq, SHIFT_P), lambda i: (i // n_inner, 0, 0)),
                  pl.BlockSpec((1, SHIFT_P), c1), vec, pl.BlockSpec((LANE, GROUP_W), c1), vec,
                  pl.BlockSpec((LANE, GROUP_W), c1), vec, vec, pl.BlockSpec((GROUP_W, GROUP_W), c1)],
        out_specs=[out] * 6,
        out_shape=[jax.ShapeDtypeStruct((t, GROUP_W), F32)] * 6,
        name="rw_prep")(p, p, first_prev, prm['mu'], prm['w0'], prm['w2'], prm['a0'], prm['a2'],
                        prm['kk'], prm['ka'], prm['ones_bd'])


def _rw_scan_body(w_ref, kk_ref, ka_ref, k_ref, r_ref, v_ref, s0_ref, y_ref, sout_ref, s_ref, *, tb, nj, fold):
    @pl.when(pl.program_id(1) == 0)
    def _():
        s_ref[...] = s0_ref[...]

    si = s_ref.shape[1]
    zero = jnp.zeros((si, LANE), F32)
    keys = range(nj)
    rowj = lambda ref, t, j: ref[t, j:j + 1, :]

    def token_folded(t, carry):
        sa = zero
        for j in keys:
            sa = sa + s_ref[j] * rowj(kk_ref, t, j)
        sa = -(sa + pltpu.roll(sa, LANE // 2, 1))
        vt = v_ref[t]
        for j in keys:
            s_ref[j] = s_ref[j] * rowj(w_ref, t, j) + vt * rowj(k_ref, t, j)
        y = zero
        for j in keys:
            s_new = s_ref[j] + sa * rowj(ka_ref, t, j)
            s_ref[j] = s_new
            y = y + s_new * rowj(r_ref, t, j)
        y_ref[t] = y
        return carry

    def token(t, carry):
        def p1(j, sa):
            return sa + s_ref[j] * kk_ref[t, pl.ds(j, 1), :]
        sa = -lax.fori_loop(0, nj, p1, zero, unroll=8)
        vt = v_ref[t]

        def p2(j, y):
            s_new = (s_ref[j] * w_ref[t, pl.ds(j, 1), :] + sa * ka_ref[t, pl.ds(j, 1), :]
                     + vt * k_ref[t, pl.ds(j, 1), :])
            s_ref[j] = s_new
            return y + s_new * r_ref[t, pl.ds(j, 1), :]
        y_ref[t] = lax.fori_loop(0, nj, p2, zero, unroll=8)
        return carry

    lax.fori_loop(0, tb, token_folded if fold else token, 0)
    sout_ref[...] = s_ref[...]


def _rw_scan(rowvecs, v, s0, *, tb, fold):
    seq, nj, lanes = rowvecs[0].shape
    si = v.shape[1]
    ngroups = lanes // LANE
    body = functools.partial(_rw_scan_body, tb=tb, nj=nj, fold=fold)
    rv = pl.BlockSpec((tb, nj, LANE), lambda g, t: (t, 0, g))
    cv = pl.BlockSpec((tb, si, LANE), lambda g, t: (t, 0, g))
    st = pl.BlockSpec((nj, si, LANE), lambda g, t: (0, 0, g))
    return pl.pallas_call(
        body, grid=(ngroups, seq // tb),
        in_specs=[rv] * 5 + [cv, st],
        out_specs=[cv, st],
        out_shape=[jax.ShapeDtypeStruct((seq, si, lanes), F32), jax.ShapeDtypeStruct((nj, si, lanes), F32)],
        scratch_shapes=[pltpu.VMEM((nj, si, LANE), F32)],
        name="rw_scan")(*rowvecs, v, s0)


def _rw_post_body(y_ref, r_ref, k_ref, v_ref, g_ref, rk_ref, lnw_ref, lnb_ref, ones_ref, o_ref):
    ones_bd = ones_ref[...]
    y = y_ref[...]
    mu = _head_sum(y, ones_bd) * (1.0 / RW_HD)
    d = y - mu
    var = _head_sum(d * d, ones_bd) * (1.0 / RW_HD)
    yn = d * lax.rsqrt(var + RW_LN_EPS) * lnw_ref[...] + lnb_ref[...]
    v = v_ref[...]
    out = yn + _head_sum(r_ref[...] * k_ref[...] * rk_ref[...], ones_bd) * v
    o_ref[...] = (out * _silu(g_ref[...])).astype(o_ref.dtype)


def _rw_post(y, r, k, v, p, prm, rows=BLK):
    t = y.shape[0]
    blk = pl.BlockSpec((rows, GROUP_W), lambda i: (i, 0))
    vec = pl.BlockSpec((1, GROUP_W), lambda i: (0, 0))
    return pl.pallas_call(
        _rw_post_body, grid=(t // rows,),
        in_specs=[blk, blk, blk, blk, pl.BlockSpec((rows, GROUP_W), lambda i: (i, _COL['rw_g'] // 4)),
                  vec, vec, vec, pl.BlockSpec((GROUP_W, GROUP_W), lambda i: (0, 0))],
        out_specs=blk, out_shape=jax.ShapeDtypeStruct((t, GROUP_W), BF16),
        name="rw_post")(y, r, k, v, p, prm['rk'], prm['lnw'], prm['lnb'], prm['ones_bd'])


def _permute_shift(x):
    o1, o2, o3, o4 = GROUP_W, GROUP_W + RW_LORA, 2 * GROUP_W + RW_LORA, 3 * GROUP_W + RW_LORA
    z = lambda n: jnp.zeros(x.shape[:-1] + (n,), x.dtype)
    return jnp.concatenate([x[..., :o1], x[..., o2:o3], x[..., o3:o4], x[..., o1:o2], z(LANE - RW_LORA),
                            x[..., o4:], z(LANE - RW_LORA), z(2 * LANE)], axis=-1)


def _unpermute_shift(x):
    g = GROUP_W
    return jnp.concatenate([x[..., :g], x[..., 3 * g:3 * g + RW_LORA], x[..., g:2 * g], x[..., 2 * g:3 * g],
                            x[..., 3 * g + LANE:3 * g + LANE + RW_LORA]], axis=-1)


def _layout_w_in(w):
    d = w.shape[0]
    off = {}
    o = 0
    for name, n in (('ret', 3 * GROUP_W), ('sb', 4 * GROUP_W), ('dsa_q', GROUP_W), ('dsa_k', DSA_HD), ('dsa_v', DSA_HD),
                    ('idx_q', IDX_HEADS * IDX_HD), ('idx_k', IDX_HD), ('idx_w', IDX_HEADS), ('dsa_g', GROUP_W),
                    ('rw_shift', SHIFT_W), ('rw_g', GROUP_W)):
        off[name] = (o, o + n)
        o += n
    sl = lambda name: w[:, off[name][0]:off[name][1]]
    z = lambda n: jnp.zeros((d, n), w.dtype)
    parts = [sl('ret'), sl('sb'), sl('dsa_q'), sl('idx_q'), sl('dsa_g'), sl('dsa_k'), sl('dsa_v'),
             sl('idx_k'), sl('idx_k'), sl('idx_w'), z(LANE - IDX_HEADS), sl('rw_g'), _permute_shift(sl('rw_shift'))]
    out = jnp.concatenate(parts, axis=1).astype(BF16)
    assert out.shape[1] == N_COLS
    return out


def _pad_lora(w2):
    return jnp.concatenate([w2, jnp.zeros((LANE - RW_LORA, GROUP_W), w2.dtype)], axis=0).astype(BF16)


def _layer_params(l, norm_g, w_in, shift_mu, rw_w0, rw_w2, rw_a0, rw_a2, rw_kk, rw_ka, rw_rk, rw_lnx_w, rw_lnx_b, w_out):
    head = np.arange(GROUP_W) // RW_HD
    row = lambda x: x.reshape(1, -1)
    return dict(g=norm_g[l], w_in=_layout_w_in(w_in[l]), w_out=w_out[l].astype(BF16),
                mu=row(_permute_shift(shift_mu[l])), w0=row(rw_w0[l]), w2=_pad_lora(rw_w2[l]),
                a0=row(rw_a0[l]), a2=_pad_lora(rw_a2[l]), kk=row(rw_kk[l]), ka=row(rw_ka[l]),
                rk=row(rw_rk[l]), lnw=row(rw_lnx_w[l]), lnb=row(rw_lnx_b[l]),
                ones_bd=jnp.asarray(head[:, None] == head[None, :], BF16))


def _scan_prompt(prep, n, seq):
    r, w, k, v, kk, ka = prep
    half = RW_HD // 2
    nh = n * RW_HEADS
    assert 2 * nh == LANE

    def rowvec(x):
        return x.reshape(n, seq, RW_HEADS, 2, half).transpose(1, 4, 3, 0, 2).reshape(seq, half, LANE)

    def colvec(x):
        y = x.reshape(n, seq, RW_HEADS, RW_HD).transpose(1, 3, 0, 2).reshape(seq, RW_HD, nh)
        return jnp.concatenate([y, y], axis=-1)

    s0 = jnp.zeros((half, RW_HD, LANE), F32)
    y, s = _rw_scan([rowvec(w), rowvec(kk), rowvec(ka), rowvec(k), rowvec(r)], colvec(v), s0, tb=32, fold=True)
    y = (y[:, :, :nh] + y[:, :, nh:]).reshape(seq, RW_HD, n, RW_HEADS).transpose(2, 0, 3, 1).reshape(n * seq, GROUP_W)
    s = s.reshape(half, RW_HD, 2, n, RW_HEADS).transpose(3, 4, 1, 2, 0).reshape(n, RW_HEADS, RW_HD, RW_HD)
    return y, s


def _scan_sample(prep, s0, n, seq_pad, valid_len):
    r, w, k, v, kk, ka = prep

    def vec(x):
        return x.reshape(n, seq_pad, RW_HEADS, RW_HD)[:, :valid_len].transpose(1, 3, 0, 2).reshape(
            valid_len, RW_HD, n * RW_HEADS)

    s0 = s0.transpose(3, 2, 0, 1).reshape(RW_HD, RW_HD, n * RW_HEADS)
    y, s = _rw_scan([vec(w), vec(kk), vec(ka), vec(k), vec(r)], vec(v), s0, tb=valid_len, fold=False)
    y = y.reshape(valid_len, RW_HD, n, RW_HEADS).transpose(2, 0, 3, 1)
    y = jnp.pad(y, ((0, 0), (0, seq_pad - valid_len), (0, 0), (0, 0))).reshape(n * seq_pad, GROUP_W)
    s = s.reshape(RW_HD, RW_HD, n, RW_HEADS).transpose(2, 3, 1, 0)
    return y, s


def _prompt_layer(x, h, prm, n, seq, tables, last_g, last_dtype):
    nq = seq // BLK
    p = _in_proj(h, prm['w_in'])
    ret_o, ret_s = _retention(p, jnp.zeros((n, RET_HEADS * RET_DK, RET_DV), F32), tables,
                              n_seq=1, seq_len=BLK, n_outer=n, n_inner=nq, table_by_inner=True)
    sb_o = _sb_prompt(p, n, seq)
    dsa_o = _dsa_prompt(p, n, seq)
    prep = _rw_prep(p, jnp.zeros((n, 1, SHIFT_P), F32), prm, n_seq=1, seq_len=BLK, n_outer=n, n_inner=nq)
    y, wkv_s = _scan_prompt(prep, n, seq)
    rw_o = _rw_post(y, prep[0], prep[2], prep[3], p, prm)
    mix = jnp.concatenate([ret_o, sb_o, dsa_o, rw_o], axis=1)
    x, h = _out_proj(x, mix, prm['w_out'], last_g, last_dtype)
    p3 = p.reshape(n, seq, N_COLS)
    c = lambda name: _COL[name] * LANE
    sb_new = p3[:, :, c('sb_k'):c('sb_g')].reshape(n, seq, 2, SB_HEADS, SB_HD)
    dsa_new = p3[:, :, c('dsa_k'):c('dsa_k') + DSA_CACHE_W]
    shift = _unpermute_shift(p3[:, seq - 1, c('rw_shift'):])
    return x, h, (sb_new, dsa_new, ret_s.reshape(n, RET_HEADS, RET_DK, RET_DV), wkv_s, shift)


def _sample_layer(x, h, prm, n, seq_pad, valid_len, tables, past, page_table, last_g, last_dtype):
    shift_prev, ret_s0, wkv_s0, sb_cache, dsa_cache = past
    n_seq = BLK // seq_pad
    nb = n // n_seq
    p = _in_proj(h, prm['w_in'])
    ret_o, ret_s = _retention(p, ret_s0.reshape(n, RET_HEADS * RET_DK, RET_DV), tables,
                              n_seq=n_seq, seq_len=seq_pad, n_outer=nb, n_inner=1, table_by_inner=False)
    sb_o = _sb_sample(p, sb_cache, page_table, n, seq_pad, valid_len)
    dsa_o = _dsa_sample(p, dsa_cache, page_table, n, seq_pad, valid_len)
    first_prev = _permute_shift(shift_prev).reshape(nb, n_seq, SHIFT_P)
    prep = _rw_prep(p, first_prev, prm, n_seq=n_seq, seq_len=seq_pad, n_outer=nb, n_inner=1)
    y, wkv_s = _scan_sample(prep, wkv_s0, n, seq_pad, valid_len)
    rw_o = _rw_post(y, prep[0], prep[2], prep[3], p, prm)
    mix = jnp.concatenate([ret_o, sb_o, dsa_o, rw_o], axis=1)
    x, h = _out_proj(x, mix, prm['w_out'], last_g, last_dtype)
    p3 = p.reshape(n, seq_pad, N_COLS)[:, :valid_len]
    c = lambda name: _COL[name] * LANE
    sb_new = p3[:, :, c('sb_k'):c('sb_g')].reshape(n, valid_len, 2, SB_HEADS, SB_HD)
    dsa_new = p3[:, :, c('dsa_k'):c('dsa_k') + DSA_CACHE_W]
    shift = _unpermute_shift(p3[:, valid_len - 1, c('rw_shift'):])
    return x, h, (sb_new, dsa_new, ret_s.reshape(n, RET_HEADS, RET_DK, RET_DV), wkv_s, shift)


def kernel(x_prompt, x_sample, cache_sb_kv, cache_dsa, state_ret, state_wkv, state_shift, page_table,
           norm_g, w_in, shift_mu, rw_w0, rw_w2, rw_a0, rw_a2, rw_kk, rw_ka, rw_rk, rw_lnx_w, rw_lnx_b,
           w_out, final_g):
    n_p, seq, d = x_prompt.shape
    n_s, dec_seq, _ = x_sample.shape
    depth = norm_g.shape[0]
    past_len = page_table.shape[1] * cache_sb_kv.shape[2]
    assert cache_sb_kv.shape[2] == BLK and seq % BLK == 0 and d == 4 * GROUP_W
    seq_pad = SUBLANE
    assert dec_seq <= seq_pad and (n_s * seq_pad) % BLK == 0
    n_seq = BLK // seq_pad

    prms = [_layer_params(l, norm_g, w_in, shift_mu, rw_w0, rw_w2, rw_a0, rw_a2, rw_kk, rw_ka, rw_rk,
                          rw_lnx_w, rw_lnx_b, w_out) for l in range(depth)]
    pos = np.arange(seq)
    rows = np.arange(BLK)
    tables_p = _rotary_tables(pos) + _decay_tables(rows, np.zeros(BLK, np.int64), float(BLK))
    tables_s = (_rotary_tables(past_len + rows % seq_pad)
                + _decay_tables(rows % seq_pad, rows // seq_pad, float(dec_seq)))

    n_pool = cache_sb_kv.shape[1]
    sb_pages = cache_sb_kv.reshape(depth * n_pool * SB_PAGE_ROWS, SB_HD)
    dsa_pages = cache_dsa.transpose(0, 1, 3, 2).reshape(depth * n_pool, DSA_CACHE_W, BLK)
    xp = x_prompt.reshape(n_p * seq, d)
    xs = jnp.pad(x_sample, ((0, 0), (0, seq_pad - dec_seq), (0, 0))).reshape(n_s * seq_pad, d)
    hp = _rmsnorm(xp, prms[0]['g'], BF16)
    hs = _rmsnorm(xs, prms[0]['g'], BF16)
    outs_p, outs_s = [], []
    for l in range(depth):
        last = l == depth - 1
        g_next = final_g if last else prms[l + 1]['g']
        dt = F32 if last else BF16
        xp, hp, st = _prompt_layer(xp, hp, prms[l], n_p, seq, tables_p, g_next, dt)
        outs_p.append(st)
        past = (state_shift[l], state_ret[l], state_wkv[l], sb_pages, dsa_pages)
        xs, hs, st = _sample_layer(xs, hs, prms[l], n_s, seq_pad, dec_seq, tables_s, past,
                                   page_table + l * n_pool, g_next, dt)
        outs_s.append(st)
    y_prompt = hp.reshape(n_p, seq, d)
    y_sample = hs.reshape(n_s, seq_pad, d)[:, :dec_seq]
    stack = lambda outs, i: jnp.stack([o[i] for o in outs])
    return (y_prompt, y_sample, *[stack(outs_p, i) for i in range(5)], *[stack(outs_s, i) for i in range(5)])
```
